```python
import jax, jax.numpy as jnp
from jax import lax
import numpy as np

D_MODEL = 1024
BATCH = 32
SEQ = 2048
DEPTH = 1

CHUNK = 64
Q_BLOCK = 128
SB_HEADS = 8
SB_HEAD_DIM = 64
SB_WIDTH = SB_HEADS * SB_HEAD_DIM
RW_HEADS = 8
RW_HEAD_DIM = 64
RW_WIDTH = RW_HEADS * RW_HEAD_DIM
W_LORA = 64
A_LORA = 64
G_LORA = 128
GN_EPS = RW_HEAD_DIM * 1e-5
N_BRANCH = 2
SB_COLS = 3 * SB_WIDTH
RW_COLS = 3 * RW_WIDTH + W_LORA + A_LORA + G_LORA
GATE_COLS = N_BRANCH * D_MODEL
IN_COLS = SB_COLS + RW_COLS + GATE_COLS
D_FF = ((8 * D_MODEL + 3 * 256 - 1) // (3 * 256)) * 256
RMS_EPS = 1e-6

kernel_name = "hybrid_stickbreak_rwkv7_swiglu_sandwich"


def rms_norm(x, g):
    xf = x.astype(jnp.float32)
    y = xf * lax.rsqrt(jnp.mean(xf * xf, axis=-1, keepdims=True) + RMS_EPS)
    return (y * g).astype(x.dtype)


def token_shift(p):
    return jnp.pad(p, ((0, 0), (1, 0), (0, 0)))[:, :-1]


def stick_breaking_attention(q, k, v):
    B, S, H, Dh = q.shape
    scale = Dh ** -0.5
    outs = []
    for i in range(S // Q_BLOCK):
        q0 = i * Q_BLOCK
        kv_len = q0 + Q_BLOCK
        qb = q[:, q0:kv_len]
        kb = k[:, :kv_len]
        vb = v[:, :kv_len]
        z = jnp.einsum('bqhd,bkhd->bhqk', qb, kb).astype(jnp.float32) * scale
        t_idx = q0 + jnp.arange(Q_BLOCK)[:, None]
        s_idx = jnp.arange(kv_len)[None, :]
        strict = s_idx < t_idx
        log_fail = jnp.where(strict, jax.nn.log_sigmoid(-z), 0.0)
        after = lax.cumsum(log_fail, axis=3, reverse=True) - log_fail
        w = jnp.where(strict, jnp.exp(jax.nn.log_sigmoid(z) + after), 0.0)
        outs.append(jnp.einsum('bhqk,bkhd->bqhd', w.astype(vb.dtype), vb))
    return jnp.concatenate(outs, axis=1)


def wkv7_scan(r, decay, k, v, kk, a):
    B, S, H, N = r.shape

    def to_chunks(t):
        return t.astype(jnp.float32).transpose(1, 0, 2, 3).reshape(S // CHUNK, CHUNK, B, H, N)

    def step(state, inp):
        r_t, w_t, k_t, v_t, kk_t, a_t = inp
        sa = jnp.einsum('bhvk,bhk->bhv', state, -kk_t)
        state = (state * w_t[:, :, None, :]
                 + sa[..., None] * (kk_t * a_t)[:, :, None, :]
                 + v_t[..., None] * k_t[:, :, None, :])
        return state, jnp.einsum('bhvk,bhk->bhv', state, r_t)

    def chunk_step(state, chunk_inp):
        return lax.scan(step, state, chunk_inp)

    state0 = jnp.zeros((B, H, N, N), jnp.float32)
    _, y = lax.scan(chunk_step, state0, tuple(to_chunks(t) for t in (r, decay, k, v, kk, a)))
    return y.reshape(S, B, H, N).transpose(1, 0, 2, 3)


def rwkv7_time_mix(p, mu, w0, w_up, a0, a_up, g_up, k_k, k_a, r_k, lnx_w, lnx_b):
    B, S, _ = p.shape
    H, N = RW_HEADS, RW_HEAD_DIM
    p = p + (token_shift(p) - p) * mu
    o1, o2, o3 = RW_WIDTH, 2 * RW_WIDTH, 3 * RW_WIDTH
    r, k, v = p[..., :o1], p[..., o1:o2], p[..., o2:o3]
    xw = p[..., o3:o3 + W_LORA]
    xa = p[..., o3 + W_LORA:o3 + W_LORA + A_LORA]
    xg = p[..., o3 + W_LORA + A_LORA:]
    w_raw = (w0 + jnp.tanh(xw) @ w_up).astype(jnp.float32)
    decay = jnp.exp(-jnp.exp(-jax.nn.softplus(-w_raw) - 0.5))
    a = jax.nn.sigmoid((a0 + xa @ a_up).astype(jnp.float32))
    g = jax.nn.sigmoid(xg) @ g_up
    kk = (k * k_k).astype(jnp.float32).reshape(B, S, H, N)
    kk = kk / jnp.maximum(jnp.linalg.norm(kk, axis=-1, keepdims=True), 1e-12)
    k = k.astype(jnp.float32) * (1.0 + (a - 1.0) * k_a)
    heads = lambda t: t.astype(jnp.float32).reshape(B, S, H, N)
    r_h, k_h, v_h = heads(r), heads(k), heads(v)
    y = wkv7_scan(r_h, heads(decay), k_h, v_h, kk, heads(a))
    mean = jnp.mean(y, axis=-1, keepdims=True)
    var = jnp.mean(jnp.square(y - mean), axis=-1, keepdims=True)
    y = ((y - mean) * lax.rsqrt(var + GN_EPS)).reshape(B, S, RW_WIDTH) * lnx_w + lnx_b
    bonus = (jnp.sum(r_h * k_h * r_k, axis=-1, keepdims=True) * v_h).reshape(B, S, RW_WIDTH)
    return ((y + bonus) * g).astype(p.dtype)


def setup_inputs(seed: int = 0) -> dict:
    key = jax.random.key(seed)
    ks = jax.random.split(key, 24)
    L, D = DEPTH, D_MODEL
    nrm = lambda k_, shape, s: jax.random.normal(k_, shape, jnp.float32) * s
    return {
        "x": nrm(ks[0], (BATCH, SEQ, D), 1.0),
        "norm_mix_pre": 1.0 + nrm(ks[1], (L, D), 0.1),
        "w_in": nrm(ks[2], (L, D, IN_COLS), D ** -0.5),
        "b_gate": nrm(ks[3], (L, GATE_COLS), 0.1),
        "mu_rw": jax.random.uniform(ks[4], (L, RW_COLS), jnp.float32),
        "w0": jax.random.uniform(ks[5], (L, RW_WIDTH), jnp.float32, -6.0, 0.0),
        "w_up": nrm(ks[6], (L, W_LORA, RW_WIDTH), 0.1),
        "a0": nrm(ks[7], (L, RW_WIDTH), 0.1),
        "a_up": nrm(ks[8], (L, A_LORA, RW_WIDTH), 0.5 * A_LORA ** -0.5),
        "g_up": nrm(ks[9], (L, G_LORA, RW_WIDTH), G_LORA ** -0.5),
        "k_k": 0.85 + nrm(ks[10], (L, RW_WIDTH), 0.05),
        "k_a": 1.0 + nrm(ks[11], (L, RW_WIDTH), 0.05),
        "r_k": nrm(ks[12], (L, RW_HEADS, RW_HEAD_DIM), 0.1),
        "lnx_w": 1.0 + nrm(ks[13], (L, RW_WIDTH), 0.1),
        "lnx_b": nrm(ks[14], (L, RW_WIDTH), 0.02),
        "w_sb_out": nrm(ks[15], (L, SB_WIDTH, D), SB_WIDTH ** -0.5),
        "w_rw_out": nrm(ks[16], (L, RW_WIDTH, D), RW_WIDTH ** -0.5),
        "w_o": nrm(ks[17], (L, D, D), D ** -0.5),
        "norm_mix_post": 1.0 + nrm(ks[18], (L, D), 0.1),
        "norm_ffn_pre": 1.0 + nrm(ks[19], (L, D), 0.1),
        "w_ffn_gate": nrm(ks[20], (L, D, D_FF), D ** -0.5),
        "w_ffn_up": nrm(ks[21], (L, D, D_FF), D ** -0.5),
        "w_ffn_down": nrm(ks[22], (L, D_FF, D), D_FF ** -0.5),
        "norm_ffn_post": 1.0 + nrm(ks[23], (L, D), 0.1),
    }


def reference(x, norm_mix_pre, w_in, b_gate, mu_rw, w0, w_up, a0, a_up, g_up, k_k, k_a, r_k,
              lnx_w, lnx_b, w_sb_out, w_rw_out, w_o, norm_mix_post, norm_ffn_pre,
              w_ffn_gate, w_ffn_up, w_ffn_down, norm_ffn_post):
    B, S, D = x.shape
    for l in range(DEPTH):
        h = rms_norm(x, norm_mix_pre[l])
        proj = h @ w_in[l]
        p_sb = proj[..., :SB_COLS]
        p_rw = proj[..., SB_COLS:SB_COLS + RW_COLS]
        gates = jax.nn.sigmoid(proj[..., SB_COLS + RW_COLS:] + b_gate[l])
        q = p_sb[..., :SB_WIDTH].reshape(B, S, SB_HEADS, SB_HEAD_DIM)
        k = p_sb[..., SB_WIDTH:2 * SB_WIDTH].reshape(B, S, SB_HEADS, SB_HEAD_DIM)
        v = p_sb[..., 2 * SB_WIDTH:].reshape(B, S, SB_HEADS, SB_HEAD_DIM)
        o_sb = stick_breaking_attention(q, k, v).reshape(B, S, SB_WIDTH)
        o_rw = rwkv7_time_mix(p_rw, mu_rw[l], w0[l], w_up[l], a0[l], a_up[l], g_up[l],
                              k_k[l], k_a[l], r_k[l], lnx_w[l], lnx_b[l])
        merged = (gates[..., :D] * (o_sb @ w_sb_out[l])
                  + gates[..., D:] * (o_rw @ w_rw_out[l]))
        x = x + rms_norm(merged @ w_o[l], norm_mix_post[l])
        h = rms_norm(x, norm_ffn_pre[l])
        f = (jax.nn.silu(h @ w_ffn_gate[l]) * (h @ w_ffn_up[l])) @ w_ffn_down[l]
        x = x + rms_norm(f, norm_ffn_post[l])
    return x
```

```python
import functools

import jax
import jax.numpy as jnp
from jax import lax
from jax.experimental import pallas as pl
from jax.experimental.pallas import tpu as pltpu

F32 = jnp.float32
BF16 = jnp.bfloat16

D_MODEL = 1024
HEADS = 8
HEAD_DIM = 64
WIDTH = HEADS * HEAD_DIM
W_LORA = 64
A_LORA = 64
G_LORA = 128
SB_COLS = 3 * WIDTH
RW_COLS = 3 * WIDTH + W_LORA + A_LORA + G_LORA
MIX_COLS = SB_COLS + RW_COLS
D_FF = 2816
RMS_EPS = 1e-6
GN_EPS = HEAD_DIM * 1e-5
Q_BLOCK = 128
LANES = 128
SUBLANES = 8
MXU_N = 256
VMEM_LIMIT = 56 * 1024 * 1024


def _rms(x, g):
    ms = jnp.mean(x * x, axis=-1, keepdims=True)
    return x * lax.rsqrt(ms + RMS_EPS) * g


def _sigmoid(x):
    return 1.0 / (1.0 + jnp.exp(-x))


def _softplus(x):
    return jnp.maximum(x, 0.0) + jnp.log(1.0 + jnp.exp(-jnp.abs(x)))


def _in_proj_kernel(x_ref, g_ref, w_ref, qkv_ref, prw_ref, h_scr):
    h_scr[...] = _rms(x_ref[...], g_ref[...]).astype(BF16)
    for c0 in range(0, MIX_COLS, MXU_N):
        acc = jnp.dot(h_scr[...], w_ref[:, c0:c0 + MXU_N], preferred_element_type=F32)
        if c0 < WIDTH:
            acc = acc * (HEAD_DIM ** -0.5)
        if c0 < SB_COLS:
            qkv_ref[:, c0:c0 + MXU_N] = acc.astype(BF16)
        else:
            prw_ref[:, c0 - SB_COLS:c0 - SB_COLS + MXU_N] = acc


def _in_proj(xf, g, w, tm):
    T = xf.shape[0]
    return pl.pallas_call(
        _in_proj_kernel,
        grid=(T // tm,),
        in_specs=[
            pl.BlockSpec((tm, D_MODEL), lambda i: (i, 0)),
            pl.BlockSpec((1, D_MODEL), lambda i: (0, 0)),
            pl.BlockSpec((D_MODEL, MIX_COLS), lambda i: (0, 0)),
        ],
        out_specs=[
            pl.BlockSpec((tm, SB_COLS), lambda i: (i, 0)),
            pl.BlockSpec((tm, RW_COLS), lambda i: (i, 0)),
        ],
        out_shape=[
            jax.ShapeDtypeStruct((T, SB_COLS), BF16),
            jax.ShapeDtypeStruct((T, RW_COLS), F32),
        ],
        scratch_shapes=[pltpu.VMEM((tm, D_MODEL), BF16)],
        compiler_params=pltpu.CompilerParams(
            dimension_semantics=("parallel",), vmem_limit_bytes=VMEM_LIMIT),
        name="in_proj",
    )(xf, g, w)


def _sb_kernel(q_ref, k_ref, v_ref, o_ref):
    S = q_ref.shape[1]
    nq = S // Q_BLOCK
    row = lax.broadcasted_iota(jnp.int32, (Q_BLOCK, Q_BLOCK), 0)
    col = lax.broadcasted_iota(jnp.int32, (Q_BLOCK, Q_BLOCK), 1)
    after_mat = jnp.where(row > col, 1.0, 0.0).astype(BF16)
    after_mat2 = jnp.concatenate([after_mat, after_mat], axis=0)
    strict = col < row
    head0 = col < HEAD_DIM
    zero = jnp.zeros((Q_BLOCK, Q_BLOCK), F32)

    def block(qh, kb, vb, carry, acc, diag):
        z = lax.dot_general(qh, kb, (((1,), (1,)), ((), ())), preferred_element_type=F32)
        sp = _softplus(z)
        lf = -sp
        if diag:
            lf = jnp.where(strict, lf, 0.0)
        hi = lf.astype(BF16)
        lo = (lf - hi.astype(F32)).astype(BF16)
        after = jnp.dot(jnp.concatenate([hi, lo], axis=1), after_mat2,
                        preferred_element_type=F32) + carry
        w = jnp.exp(z - sp + after)
        if diag:
            w = jnp.where(strict, w, 0.0)
        acc = acc + jnp.dot(w.astype(BF16), vb, preferred_element_type=F32)
        carry = carry + jnp.sum(lf, axis=1, keepdims=True)
        return carry, acc

    def q_body(i, _):
        q0 = pl.multiple_of(i * Q_BLOCK, Q_BLOCK)
        q = q_ref[0, pl.ds(q0, Q_BLOCK), :]
        qa = jnp.where(head0, q, jnp.zeros_like(q))
        qb = jnp.where(head0, jnp.zeros_like(q), q)
        kd = k_ref[0, pl.ds(q0, Q_BLOCK), :]
        vd = v_ref[0, pl.ds(q0, Q_BLOCK), :]
        ca, aa = block(qa, kd, vd, zero, zero, True)
        cb, ab = block(qb, kd, vd, zero, zero, True)

        def k_body(jj, st):
            ca, aa, cb, ab = st
            k0 = pl.multiple_of((i - 1 - jj) * Q_BLOCK, Q_BLOCK)
            kb = k_ref[0, pl.ds(k0, Q_BLOCK), :]
            vb = v_ref[0, pl.ds(k0, Q_BLOCK), :]
            ca, aa = block(qa, kb, vb, ca, aa, False)
            cb, ab = block(qb, kb, vb, cb, ab, False)
            return ca, aa, cb, ab

        ca, aa, cb, ab = lax.fori_loop(0, i, k_body, (ca, aa, cb, ab))
        o_ref[0, pl.ds(q0, Q_BLOCK), :] = jnp.where(head0, aa, ab).astype(o_ref.dtype)
        return 0

    lax.fori_loop(0, nq, q_body, 0)


def _sb_attn(qkv):
    B, S, _ = qkv.shape
    npair = WIDTH // LANES
    return pl.pallas_call(
        _sb_kernel,
        grid=(B, npair),
        in_specs=[
            pl.BlockSpec((1, S, LANES), lambda b, p: (b, 0, p)),
            pl.BlockSpec((1, S, LANES), lambda b, p: (b, 0, npair + p)),
            pl.BlockSpec((1, S, LANES), lambda b, p: (b, 0, 2 * npair + p)),
        ],
        out_specs=pl.BlockSpec((1, S, LANES), lambda b, p: (b, 0, p)),
        out_shape=jax.ShapeDtypeStruct((B, S, WIDTH), BF16),
        compiler_params=pltpu.CompilerParams(
            dimension_semantics=("parallel", "parallel"), vmem_limit_bytes=VMEM_LIMIT),
        name="sb_attn",
    )(qkv, qkv, qkv)


def _rw_prep_kernel(p_ref, pp_ref, mu_ref, w0_ref, a0_ref, wwa_ref, gup_ref, kk_ref, ka_ref,
                    out_ref, g_ref):
    i = pl.program_id(1)
    cur = p_ref[0]
    ts = cur.shape[0]
    last_prev = jnp.where(i > 0, pp_ref[0, SUBLANES - 1:SUBLANES, :], 0.0)
    rowid = lax.broadcasted_iota(jnp.int32, cur.shape, 0)
    prev = jnp.where(rowid == 0, last_prev, pltpu.roll(cur, 1, axis=0))
    p = cur + (prev - cur) * mu_ref[...]
    r = p[:, 0:WIDTH]
    k = p[:, WIDTH:2 * WIDTH]
    v = p[:, 2 * WIDTH:3 * WIDTH]
    xwa = p[:, 3 * WIDTH:3 * WIDTH + W_LORA + A_LORA]
    xg = p[:, 3 * WIDTH + W_LORA + A_LORA:]
    lane = lax.broadcasted_iota(jnp.int32, (ts, W_LORA + A_LORA), 1)
    twa = jnp.where(lane < W_LORA, jnp.tanh(xwa), xwa)
    wa = jnp.dot(twa, wwa_ref[...], preferred_element_type=F32, precision=lax.Precision.HIGHEST)
    w_raw = w0_ref[...] + wa[:, :WIDTH]
    decay = jnp.exp(-jnp.exp(-_softplus(-w_raw) - 0.5))
    a = _sigmoid(a0_ref[...] + wa[:, WIDTH:])
    g = jnp.dot(_sigmoid(xg), gup_ref[...], preferred_element_type=F32,
                precision=lax.Precision.HIGHEST)
    out_ref[0, 0] = r
    out_ref[1, 0] = decay
    out_ref[2, 0] = k * (1.0 + (a - 1.0) * ka_ref[...])
    out_ref[3, 0] = v
    out_ref[4, 0] = k * kk_ref[...]
    out_ref[5, 0] = a
    g_ref[0] = g.astype(g_ref.dtype)


def _rw_prep(p_rw, mu, w0, a0, wwa, g_up, k_k, k_a, ts):
    B, S, _ = p_rw.shape
    row = lambda c: pl.BlockSpec((1, c), lambda b, i: (0, 0))
    return pl.pallas_call(
        _rw_prep_kernel,
        grid=(B, S // ts),
        in_specs=[
            pl.BlockSpec((1, ts, RW_COLS), lambda b, i: (b, i, 0)),
            pl.BlockSpec((1, SUBLANES, RW_COLS),
                         lambda b, i: (b, jnp.maximum(i * (ts // SUBLANES) - 1, 0), 0)),
            row(RW_COLS), row(WIDTH), row(WIDTH),
            pl.BlockSpec((W_LORA + A_LORA, 2 * WIDTH), lambda b, i: (0, 0)),
            pl.BlockSpec((G_LORA, WIDTH), lambda b, i: (0, 0)),
            row(WIDTH), row(WIDTH),
        ],
        out_specs=[
            pl.BlockSpec((6, 1, ts, WIDTH), lambda b, i: (0, b, i, 0)),
            pl.BlockSpec((1, ts, WIDTH), lambda b, i: (b, i, 0)),
        ],
        out_shape=[
            jax.ShapeDtypeStruct((6, B, S, WIDTH), F32),
            jax.ShapeDtypeStruct((B, S, WIDTH), BF16),
        ],
        compiler_params=pltpu.CompilerParams(
            dimension_semantics=("parallel", "parallel"), vmem_limit_bytes=VMEM_LIMIT),
        name="rw_prep",
    )(p_rw, p_rw, mu, w0, a0, wwa, g_up, k_k, k_a)


def _wkv_kernel(r_ref, w_ref, k_ref, v_ref, kk_ref, a_ref, lw_ref, lb_ref, rk_ref, o_ref,
                st_ref, y_scr):
    @pl.when(pl.program_id(1) == 0)
    def _():
        st_ref[...] = jnp.zeros_like(st_ref)

    tc = r_ref.shape[0]

    def step(t, _):
        kk = kk_ref[t]
        norm = jnp.sqrt(jnp.sum(kk * kk, axis=0, keepdims=True))
        kkn = kk * (1.0 / jnp.maximum(norm, 1e-12))
        nkk = -kkn
        bb = kkn * a_ref[t]
        w = w_ref[t]
        k = k_ref[t]
        r = r_ref[t]
        vt = v_ref[t]
        for vb in range(HEAD_DIM // SUBLANES):
            rows = []
            for j in range(SUBLANES):
                vi = vb * SUBLANES + j
                s = st_ref[vi]
                sa = jnp.sum(s * nkk, axis=0, keepdims=True)
                s = s * w + sa * bb + vt[vi:vi + 1] * k
                st_ref[vi] = s
                rows.append(jnp.sum(s * r, axis=0, keepdims=True))
            y_scr[vb * SUBLANES:(vb + 1) * SUBLANES, :] = jnp.concatenate(rows, axis=0)
        y = y_scr[...]
        d = y - jnp.mean(y, axis=0, keepdims=True)
        var = jnp.mean(d * d, axis=0, keepdims=True)
        bonus = jnp.sum(r * k * rk_ref[...], axis=0, keepdims=True) * vt
        o_ref[t] = d * lax.rsqrt(var + GN_EPS) * lw_ref[...] + lb_ref[...] + bonus
        return 0

    lax.fori_loop(0, tc, step, 0)


def _wkv(rw6, lw, lb, rk, tc):
    _, S, N, NS = rw6.shape
    inp = lambda n: pl.BlockSpec((None, tc, N, LANES), lambda g, c: (n, c, 0, g))
    const = pl.BlockSpec((N, LANES), lambda g, c: (0, g))
    return pl.pallas_call(
        _wkv_kernel,
        grid=(NS // LANES, S // tc),
        in_specs=[inp(0), inp(1), inp(2), inp(3), inp(4), inp(5), const, const, const],
        out_specs=pl.BlockSpec((tc, N, LANES), lambda g, c: (c, 0, g)),
        out_shape=jax.ShapeDtypeStruct((S, N, NS), F32),
        scratch_shapes=[pltpu.VMEM((N, N, LANES), F32), pltpu.VMEM((N, LANES), F32)],
        compiler_params=pltpu.CompilerParams(
            dimension_semantics=("parallel", "arbitrary"), vmem_limit_bytes=VMEM_LIMIT),
        name="wkv",
    )(rw6, rw6, rw6, rw6, rw6, rw6, lw, lb, rk)


def _merge_kernel(x_ref, osb_ref, y_ref, g_ref, gpre_ref, wg_ref, bg_ref, wsb_ref, wrw_ref,
                  wo_ref, gpost_ref, x1_ref, h_scr, orw_scr, m_scr, f_scr):
    x = x_ref[...]
    h_scr[...] = _rms(x, gpre_ref[...]).astype(BF16)
    orw_scr[...] = (y_ref[...] * g_ref[...].astype(F32)).astype(BF16)
    for c0 in range(0, D_MODEL, MXU_N):
        cs = slice(c0, c0 + MXU_N)
        cr = slice(D_MODEL + c0, D_MODEL + c0 + MXU_N)
        g_sb = _sigmoid(jnp.dot(h_scr[...], wg_ref[:, cs], preferred_element_type=F32)
                        + bg_ref[:, cs])
        g_rw = _sigmoid(jnp.dot(h_scr[...], wg_ref[:, cr], preferred_element_type=F32)
                        + bg_ref[:, cr])
        a_sb = jnp.dot(osb_ref[...], wsb_ref[:, cs], preferred_element_type=F32)
        a_rw = jnp.dot(orw_scr[...], wrw_ref[:, cs], preferred_element_type=F32)
        m_scr[:, cs] = (g_sb * a_sb + g_rw * a_rw).astype(BF16)
    for c0 in range(0, D_MODEL, MXU_N):
        cs = slice(c0, c0 + MXU_N)
        f_scr[:, cs] = jnp.dot(m_scr[...], wo_ref[:, cs], preferred_element_type=F32)
    x1_ref[...] = x + _rms(f_scr[...], gpost_ref[...])


def _merge(xf, o_sb, y, g, gpre, wg, bg, wsb, wrw, wo, gpost, tm):
    T = xf.shape[0]
    tile = lambda c: pl.BlockSpec((tm, c), lambda i: (i, 0))
    full = lambda a: pl.BlockSpec(a.shape, lambda i: (0, 0))
    return pl.pallas_call(
        _merge_kernel,
        grid=(T // tm,),
        in_specs=[tile(D_MODEL), tile(WIDTH), tile(WIDTH), tile(WIDTH), full(gpre), full(wg),
                  full(bg), full(wsb), full(wrw), full(wo), full(gpost)],
        out_specs=tile(D_MODEL),
        out_shape=jax.ShapeDtypeStruct((T, D_MODEL), F32),
        scratch_shapes=[pltpu.VMEM((tm, D_MODEL), BF16), pltpu.VMEM((tm, WIDTH), BF16),
                        pltpu.VMEM((tm, D_MODEL), BF16), pltpu.VMEM((tm, D_MODEL), F32)],
        compiler_params=pltpu.CompilerParams(
            dimension_semantics=("parallel",), vmem_limit_bytes=VMEM_LIMIT),
        name="merge",
    )(xf, o_sb, y, g, gpre, wg, bg, wsb, wrw, wo, gpost)


def _ffn_kernel(x_ref, gpre_ref, wg_ref, wu_ref, wd_ref, gpost_ref, o_ref, h_scr, acc_scr):
    x = x_ref[...]
    h_scr[...] = _rms(x, gpre_ref[...]).astype(BF16)
    for c0 in range(0, D_FF, MXU_N):
        cs = slice(c0, c0 + MXU_N)
        gt = jnp.dot(h_scr[...], wg_ref[:, cs], preferred_element_type=F32)
        up = jnp.dot(h_scr[...], wu_ref[:, cs], preferred_element_type=F32)
        act = (gt * _sigmoid(gt) * up).astype(BF16)
        part = jnp.dot(act, wd_ref[cs, :], preferred_element_type=F32)
        if c0 == 0:
            acc_scr[...] = part
        else:
            acc_scr[...] += part
    o_ref[...] = x + _rms(acc_scr[...], gpost_ref[...])


def _ffn(x1, gpre, wg, wu, wd, gpost, tm):
    T = x1.shape[0]
    tile = pl.BlockSpec((tm, D_MODEL), lambda i: (i, 0))
    full = lambda a: pl.BlockSpec(a.shape, lambda i: (0, 0))
    return pl.pallas_call(
        _ffn_kernel,
        grid=(T // tm,),
        in_specs=[tile, full(gpre), full(wg), full(wu), full(wd), full(gpost)],
        out_specs=tile,
        out_shape=jax.ShapeDtypeStruct((T, D_MODEL), F32),
        scratch_shapes=[pltpu.VMEM((tm, D_MODEL), BF16), pltpu.VMEM((tm, D_MODEL), F32)],
        compiler_params=pltpu.CompilerParams(
            dimension_semantics=("parallel",), vmem_limit_bytes=VMEM_LIMIT),
        name="ffn",
    )(x1, gpre, wg, wu, wd, gpost)


def _layer(x, norm_mix_pre, w_in, b_gate, mu_rw, w0, w_up, a0, a_up, g_up, k_k, k_a, r_k,
           lnx_w, lnx_b, w_sb_out, w_rw_out, w_o, norm_mix_post, norm_ffn_pre,
           w_ffn_gate, w_ffn_up, w_ffn_down, norm_ffn_post):
    B, S, D = x.shape
    T = B * S
    NS = B * HEADS
    assert D == D_MODEL and S % Q_BLOCK == 0 and NS % LANES == 0
    tm = min(512, T)
    row = lambda a: a.reshape(1, -1)

    xf = x.reshape(T, D)
    w_in_b = w_in.astype(BF16)
    qkv, p_rw = _in_proj(xf, row(norm_mix_pre), w_in_b[:, :MIX_COLS], tm)

    o_sb = _sb_attn(qkv.reshape(B, S, SB_COLS))

    wwa = jnp.zeros((W_LORA + A_LORA, 2 * WIDTH), F32)
    wwa = wwa.at[:W_LORA, :WIDTH].set(w_up).at[W_LORA:, WIDTH:].set(a_up)
    rw6, g = _rw_prep(p_rw.reshape(B, S, RW_COLS), row(mu_rw), row(w0), row(a0), wwa, g_up,
                      row(k_k), row(k_a), min(256, S))

    rw6 = rw6.reshape(6, B, S, HEADS, HEAD_DIM).transpose(0, 2, 4, 1, 3).reshape(
        6, S, HEAD_DIM, NS)
    per_stream = lambda a: jnp.tile(a.reshape(HEADS, HEAD_DIM).T, (1, B))
    y = _wkv(rw6, per_stream(lnx_w), per_stream(lnx_b), per_stream(r_k), min(64, S))
    y = y.reshape(S, HEAD_DIM, B, HEADS).transpose(2, 0, 3, 1).reshape(T, WIDTH)

    x1 = _merge(xf, o_sb.reshape(T, WIDTH), y, g.reshape(T, WIDTH), row(norm_mix_pre),
                w_in_b[:, MIX_COLS:], row(b_gate), w_sb_out.astype(BF16),
                w_rw_out.astype(BF16), w_o.astype(BF16), row(norm_mix_post), tm)
    out = _ffn(x1, row(norm_ffn_pre), w_ffn_gate.astype(BF16), w_ffn_up.astype(BF16),
               w_ffn_down.astype(BF16), row(norm_ffn_post), tm)
    return out.reshape(B, S, D)


def kernel(x, norm_mix_pre, w_in, b_gate, mu_rw, w0, w_up, a0, a_up, g_up, k_k, k_a, r_k,
           lnx_w, lnx_b, w_sb_out, w_rw_out, w_o, norm_mix_post, norm_ffn_pre,
           w_ffn_gate, w_ffn_up, w_ffn_down, norm_ffn_post):
    params = (norm_mix_pre, w_in, b_gate, mu_rw, w0, w_up, a0, a_up, g_up, k_k, k_a, r_k,
              lnx_w, lnx_b, w_sb_out, w_rw_out, w_o, norm_mix_post, norm_ffn_pre,
              w_ffn_gate, w_ffn_up, w_ffn_down, norm_ffn_post)
    for l in range(norm_mix_pre.shape[0]):
        x = _layer(x, *(p[l] for p in params))
    return x
```

```python
import functools

import jax
import jax.numpy as jnp
from jax import lax
from jax.experimental import pallas as pl
from jax.experimental.pallas import tpu as pltpu

F32 = jnp.float32
BF16 = jnp.bfloat16

D_MODEL = 1024
HEADS = 8
HEAD_DIM = 64
WIDTH = HEADS * HEAD_DIM
W_LORA = 64
A_LORA = 64
G_LORA = 128
SB_COLS = 3 * WIDTH
RW_COLS = 3 * WIDTH + W_LORA + A_LORA + G_LORA
MIX_COLS = SB_COLS + RW_COLS
D_FF = 2816
RMS_EPS = 1e-6
GN_EPS = HEAD_DIM * 1e-5
SB_TILE = 256
SB_SLOTS = 3
LOG2E = 1.4426950408889634
LANES = 128
SUBLANES = 8
MXU_N = 256
VMEM_LIMIT = 56 * 1024 * 1024


def _rms(x, g):
    ms = jnp.mean(x * x, axis=-1, keepdims=True)
    return x * lax.rsqrt(ms + RMS_EPS) * g


def _sigmoid(x):
    return 1.0 / (1.0 + jnp.exp(-x))


def _softplus(x):
    return jnp.maximum(x, 0.0) + jnp.log(1.0 + jnp.exp(-jnp.abs(x)))


def _in_proj_kernel(x_ref, g_ref, w_ref, qkv_ref, prw_ref, h_scr):
    h_scr[...] = _rms(x_ref[...], g_ref[...]).astype(BF16)
    for c0 in range(0, MIX_COLS, MXU_N):
        acc = jnp.dot(h_scr[...], w_ref[:, c0:c0 + MXU_N], preferred_element_type=F32)
        if c0 < WIDTH:
            acc = acc * (LOG2E * HEAD_DIM ** -0.5)
        if c0 < SB_COLS:
            qkv_ref[:, c0:c0 + MXU_N] = acc.astype(BF16)
        else:
            prw_ref[:, c0 - SB_COLS:c0 - SB_COLS + MXU_N] = acc


def _in_proj(xf, g, w, tm):
    T = xf.shape[0]
    return pl.pallas_call(
        _in_proj_kernel,
        grid=(T // tm,),
        in_specs=[
            pl.BlockSpec((tm, D_MODEL), lambda i: (i, 0)),
            pl.BlockSpec((1, D_MODEL), lambda i: (0, 0)),
            pl.BlockSpec((D_MODEL, MIX_COLS), lambda i: (0, 0)),
        ],
        out_specs=[
            pl.BlockSpec((tm, SB_COLS), lambda i: (i, 0)),
            pl.BlockSpec((tm, RW_COLS), lambda i: (i, 0)),
        ],
        out_shape=[
            jax.ShapeDtypeStruct((T, SB_COLS), BF16),
            jax.ShapeDtypeStruct((T, RW_COLS), F32),
        ],
        scratch_shapes=[pltpu.VMEM((tm, D_MODEL), BF16)],
        compiler_params=pltpu.CompilerParams(
            dimension_semantics=("parallel",), vmem_limit_bytes=VMEM_LIMIT),
        name="in_proj",
    )(xf, g, w)


def _sb_kernel(ti_ref, tj_ref, q_ref, k_ref, v_ref, o_ref, z_scr, c_scr, acc_scr, car_scr):
    S = q_ref.shape[1]
    nq = S // SB_TILE
    n_off = nq * (nq - 1) // 2
    row = lax.broadcasted_iota(jnp.int32, (SB_TILE, SB_TILE), 0)
    col = lax.broadcasted_iota(jnp.int32, (SB_TILE, SB_TILE), 1)
    incl_mat = jnp.where(row >= col, 1.0, 0.0).astype(BF16)
    strict = col < row
    head0 = lax.broadcasted_iota(jnp.int32, (SB_TILE, LANES), 1) < HEAD_DIM

    def rows(ref, t):
        return ref[0, pl.ds(pl.multiple_of(t * SB_TILE, SB_TILE), SB_TILE), :]

    def issue_logits(i, j, slot):
        q = rows(q_ref, i)
        kb = rows(k_ref, j)
        for h in range(2):
            qh = jnp.where(head0 if h == 0 else ~head0, q, jnp.zeros_like(q))
            z_scr[slot, h] = lax.dot_general(qh, kb, (((1,), (1,)), ((), ())),
                                             preferred_element_type=F32)

    def cumulate(slot, diag):
        for h in range(2):
            z = z_scr[slot, h]
            s = jnp.maximum(z, 0.0) + jnp.log(1.0 + jnp.exp2(-jnp.abs(z))) * LOG2E
            if diag:
                s = jnp.where(strict, s, 0.0)
            hi = s.astype(BF16)
            lo = (s - hi.astype(F32)).astype(BF16)
            c_scr[slot, h] = (jnp.dot(hi, incl_mat, preferred_element_type=F32)
                              + jnp.dot(lo, incl_mat, preferred_element_type=F32))

    def finish(i, j, slot, diag):
        vb = rows(v_ref, j)
        for h in range(2):
            c = c_scr[slot, h]
            e = z_scr[slot, h] - c
            if not diag:
                car = car_scr[i, h]
                e = e - jnp.concatenate([car, car], axis=1)
            w = jnp.exp2(e)
            if diag:
                w = jnp.where(strict, w, 0.0)
            pv = jnp.dot(w.astype(BF16), vb, preferred_element_type=F32)
            tot = jnp.broadcast_to(c[:, 0:1], (SB_TILE, LANES))
            if diag:
                acc_scr[i, h] = pv
                car_scr[i, h] = tot
            else:
                acc_scr[i, h] += pv
                car_scr[i, h] += tot

    def run(steps, tile_of, diag):
        def step(n, slot):
            issue_logits(*tile_of(jnp.minimum(n + 1, steps - 1)), (slot + 1) % SB_SLOTS)
            finish(*tile_of(n - 1), (slot - 1) % SB_SLOTS, diag)
            cumulate(slot, diag)

        issue_logits(*tile_of(0), 0)
        issue_logits(*tile_of(min(1, steps - 1)), 1)
        cumulate(0, diag)
        groups = (steps - 1) // SB_SLOTS

        def body(g, _):
            for u in range(SB_SLOTS):
                step(1 + SB_SLOTS * g + u, (1 + u) % SB_SLOTS)
            return 0

        lax.fori_loop(0, groups, body, 0)
        for n in range(1 + SB_SLOTS * groups, steps):
            step(n, n % SB_SLOTS)
        finish(*tile_of(steps - 1), (steps - 1) % SB_SLOTS, diag)

    run(nq, lambda n: (n, n), True)
    if n_off:
        run(n_off, lambda n: (ti_ref[n], tj_ref[n]), False)
    for i in range(nq):
        o_ref[0, i * SB_TILE:(i + 1) * SB_TILE, :] = jnp.where(
            head0, acc_scr[i, 0], acc_scr[i, 1]).astype(o_ref.dtype)


def _sb_attn(qkv):
    B, S, _ = qkv.shape
    npair = WIDTH // LANES
    nq = S // SB_TILE
    pairs = [(i, j) for i in range(nq) for j in range(i - 1, -1, -1)] or [(0, 0)]
    ti = jnp.array([p[0] for p in pairs], jnp.int32)
    tj = jnp.array([p[1] for p in pairs], jnp.int32)
    return pl.pallas_call(
        _sb_kernel,
        grid_spec=pltpu.PrefetchScalarGridSpec(
            num_scalar_prefetch=2,
            grid=(B, npair),
            in_specs=[
                pl.BlockSpec((1, S, LANES), lambda b, p, ti, tj: (b, 0, p)),
                pl.BlockSpec((1, S, LANES), lambda b, p, ti, tj: (b, 0, npair + p)),
                pl.BlockSpec((1, S, LANES), lambda b, p, ti, tj: (b, 0, 2 * npair + p)),
            ],
            out_specs=pl.BlockSpec((1, S, LANES), lambda b, p, ti, tj: (b, 0, p)),
            scratch_shapes=[
                pltpu.VMEM((SB_SLOTS, 2, SB_TILE, SB_TILE), F32),
                pltpu.VMEM((SB_SLOTS, 2, SB_TILE, SB_TILE), F32),
                pltpu.VMEM((nq, 2, SB_TILE, LANES), F32),
                pltpu.VMEM((nq, 2, SB_TILE, LANES), F32),
            ],
        ),
        out_shape=jax.ShapeDtypeStruct((B, S, WIDTH), BF16),
        compiler_params=pltpu.CompilerParams(
            dimension_semantics=("parallel", "parallel"), vmem_limit_bytes=VMEM_LIMIT),
        name="sb_attn",
    )(ti, tj, qkv, qkv, qkv)


def _rw_prep_kernel(p_ref, pp_ref, mu_ref, w0_ref, a0_ref, wwa_ref, gup_ref, kk_ref, ka_ref,
                    out_ref, g_ref):
    i = pl.program_id(1)
    cur = p_ref[0]
    ts = cur.shape[0]
    last_prev = jnp.where(i > 0, pp_ref[0, SUBLANES - 1:SUBLANES, :], 0.0)
    rowid = lax.broadcasted_iota(jnp.int32, cur.shape, 0)
    prev = jnp.where(rowid == 0, last_prev, pltpu.roll(cur, 1, axis=0))
    p = cur + (prev - cur) * mu_ref[...]
    r = p[:, 0:WIDTH]
    k = p[:, WIDTH:2 * WIDTH]
    v = p[:, 2 * WIDTH:3 * WIDTH]
    xwa = p[:, 3 * WIDTH:3 * WIDTH + W_LORA + A_LORA]
    xg = p[:, 3 * WIDTH + W_LORA + A_LORA:]
    lane = lax.broadcasted_iota(jnp.int32, (ts, W_LORA + A_LORA), 1)
    twa = jnp.where(lane < W_LORA, jnp.tanh(xwa), xwa)
    wa = jnp.dot(twa, wwa_ref[...], preferred_element_type=F32, precision=lax.Precision.HIGHEST)
    w_raw = w0_ref[...] + wa[:, :WIDTH]
    decay = jnp.exp(-jnp.exp(-_softplus(-w_raw) - 0.5))
    a = _sigmoid(a0_ref[...] + wa[:, WIDTH:])
    g = jnp.dot(_sigmoid(xg), gup_ref[...], preferred_element_type=F32,
                precision=lax.Precision.HIGHEST)
    out_ref[0, 0] = r
    out_ref[1, 0] = decay
    out_ref[2, 0] = k * (1.0 + (a - 1.0) * ka_ref[...])
    out_ref[3, 0] = v
    out_ref[4, 0] = k * kk_ref[...]
    out_ref[5, 0] = a
    g_ref[0] = g.astype(g_ref.dtype)


def _rw_prep(p_rw, mu, w0, a0, wwa, g_up, k_k, k_a, ts):
    B, S, _ = p_rw.shape
    row = lambda c: pl.BlockSpec((1, c), lambda b, i: (0, 0))
    return pl.pallas_call(
        _rw_prep_kernel,
        grid=(B, S // ts),
        in_specs=[
            pl.BlockSpec((1, ts, RW_COLS), lambda b, i: (b, i, 0)),
            pl.BlockSpec((1, SUBLANES, RW_COLS),
                         lambda b, i: (b, jnp.maximum(i * (ts // SUBLANES) - 1, 0), 0)),
            row(RW_COLS), row(WIDTH), row(WIDTH),
            pl.BlockSpec((W_LORA + A_LORA, 2 * WIDTH), lambda b, i: (0, 0)),
            pl.BlockSpec((G_LORA, WIDTH), lambda b, i: (0, 0)),
            row(WIDTH), row(WIDTH),
        ],
        out_specs=[
            pl.BlockSpec((6, 1, ts, WIDTH), lambda b, i: (0, b, i, 0)),
            pl.BlockSpec((1, ts, WIDTH), lambda b, i: (b, i, 0)),
        ],
        out_shape=[
            jax.ShapeDtypeStruct((6, B, S, WIDTH), F32),
            jax.ShapeDtypeStruct((B, S, WIDTH), BF16),
        ],
        compiler_params=pltpu.CompilerParams(
            dimension_semantics=("parallel", "parallel"), vmem_limit_bytes=VMEM_LIMIT),
        name="rw_prep",
    )(p_rw, p_rw, mu, w0, a0, wwa, g_up, k_k, k_a)


def _wkv_kernel(r_ref, w_ref, k_ref, v_ref, kk_ref, a_ref, lw_ref, lb_ref, rk_ref, o_ref,
                st_ref, y_scr):
    @pl.when(pl.program_id(1) == 0)
    def _():
        st_ref[...] = jnp.zeros_like(st_ref)

    tc = r_ref.shape[0]

    def step(t, _):
        kk = kk_ref[t]
        norm = jnp.sqrt(jnp.sum(kk * kk, axis=0, keepdims=True))
        kkn = kk * (1.0 / jnp.maximum(norm, 1e-12))
        nkk = -kkn
        bb = kkn * a_ref[t]
        w = w_ref[t]
        k = k_ref[t]
        r = r_ref[t]
        vt = v_ref[t]
        for vb in range(HEAD_DIM // SUBLANES):
            rows = []
            for j in range(SUBLANES):
                vi = vb * SUBLANES + j
                s = st_ref[vi]
                sa = jnp.sum(s * nkk, axis=0, keepdims=True)
                s = s * w + sa * bb + vt[vi:vi + 1] * k
                st_ref[vi] = s
                rows.append(jnp.sum(s * r, axis=0, keepdims=True))
            y_scr[vb * SUBLANES:(vb + 1) * SUBLANES, :] = jnp.concatenate(rows, axis=0)
        y = y_scr[...]
        d = y - jnp.mean(y, axis=0, keepdims=True)
        var = jnp.mean(d * d, axis=0, keepdims=True)
        bonus = jnp.sum(r * k * rk_ref[...], axis=0, keepdims=True) * vt
        o_ref[t] = d * lax.rsqrt(var + GN_EPS) * lw_ref[...] + lb_ref[...] + bonus
        return 0

    lax.fori_loop(0, tc, step, 0)


def _wkv(rw6, lw, lb, rk, tc):
    _, S, N, NS = rw6.shape
    inp = lambda n: pl.BlockSpec((None, tc, N, LANES), lambda g, c: (n, c, 0, g))
    const = pl.BlockSpec((N, LANES), lambda g, c: (0, g))
    return pl.pallas_call(
        _wkv_kernel,
        grid=(NS // LANES, S // tc),
        in_specs=[inp(0), inp(1), inp(2), inp(3), inp(4), inp(5), const, const, const],
        out_specs=pl.BlockSpec((tc, N, LANES), lambda g, c: (c, 0, g)),
        out_shape=jax.ShapeDtypeStruct((S, N, NS), F32),
        scratch_shapes=[pltpu.VMEM((N, N, LANES), F32), pltpu.VMEM((N, LANES), F32)],
        compiler_params=pltpu.CompilerParams(
            dimension_semantics=("parallel", "arbitrary"), vmem_limit_bytes=VMEM_LIMIT),
        name="wkv",
    )(rw6, rw6, rw6, rw6, rw6, rw6, lw, lb, rk)


def _merge_kernel(x_ref, osb_ref, y_ref, g_ref, gpre_ref, wg_ref, bg_ref, wsb_ref, wrw_ref,
                  wo_ref, gpost_ref, x1_ref, h_scr, orw_scr, m_scr, f_scr):
    x = x_ref[...]
    h_scr[...] = _rms(x, gpre_ref[...]).astype(BF16)
    orw_scr[...] = (y_ref[...] * g_ref[...].astype(F32)).astype(BF16)
    for c0 in range(0, D_MODEL, MXU_N):
        cs = slice(c0, c0 + MXU_N)
        cr = slice(D_MODEL + c0, D_MODEL + c0 + MXU_N)
        g_sb = _sigmoid(jnp.dot(h_scr[...], wg_ref[:, cs], preferred_element_type=F32)
                        + bg_ref[:, cs])
        g_rw = _sigmoid(jnp.dot(h_scr[...], wg_ref[:, cr], preferred_element_type=F32)
                        + bg_ref[:, cr])
        a_sb = jnp.dot(osb_ref[...], wsb_ref[:, cs], preferred_element_type=F32)
        a_rw = jnp.dot(orw_scr[...], wrw_ref[:, cs], preferred_element_type=F32)
        m_scr[:, cs] = (g_sb * a_sb + g_rw * a_rw).astype(BF16)
    for c0 in range(0, D_MODEL, MXU_N):
        cs = slice(c0, c0 + MXU_N)
        f_scr[:, cs] = jnp.dot(m_scr[...], wo_ref[:, cs], preferred_element_type=F32)
    x1_ref[...] = x + _rms(f_scr[...], gpost_ref[...])


def _merge(xf, o_sb, y, g, gpre, wg, bg, wsb, wrw, wo, gpost, tm):
    T = xf.shape[0]
    tile = lambda c: pl.BlockSpec((tm, c), lambda i: (i, 0))
    full = lambda a: pl.BlockSpec(a.shape, lambda i: (0, 0))
    return pl.pallas_call(
        _merge_kernel,
        grid=(T // tm,),
        in_specs=[tile(D_MODEL), tile(WIDTH), tile(WIDTH), tile(WIDTH), full(gpre), full(wg),
                  full(bg), full(wsb), full(wrw), full(wo), full(gpost)],
        out_specs=tile(D_MODEL),
        out_shape=jax.ShapeDtypeStruct((T, D_MODEL), F32),
        scratch_shapes=[pltpu.VMEM((tm, D_MODEL), BF16), pltpu.VMEM((tm, WIDTH), BF16),
                        pltpu.VMEM((tm, D_MODEL), BF16), pltpu.VMEM((tm, D_MODEL), F32)],
        compiler_params=pltpu.CompilerParams(
            dimension_semantics=("parallel",), vmem_limit_bytes=VMEM_LIMIT),
        name="merge",
    )(xf, o_sb, y, g, gpre, wg, bg, wsb, wrw, wo, gpost)


def _ffn_kernel(x_ref, gpre_ref, wg_ref, wu_ref, wd_ref, gpost_ref, o_ref, h_scr, acc_scr):
    x = x_ref[...]
    h_scr[...] = _rms(x, gpre_ref[...]).astype(BF16)
    for c0 in range(0, D_FF, MXU_N):
        cs = slice(c0, c0 + MXU_N)
        gt = jnp.dot(h_scr[...], wg_ref[:, cs], preferred_element_type=F32)
        up = jnp.dot(h_scr[...], wu_ref[:, cs], preferred_element_type=F32)
        act = (gt * _sigmoid(gt) * up).astype(BF16)
        part = jnp.dot(act, wd_ref[cs, :], preferred_element_type=F32)
        if c0 == 0:
            acc_scr[...] = part
        else:
            acc_scr[...] += part
    o_ref[...] = x + _rms(acc_scr[...], gpost_ref[...])


def _ffn(x1, gpre, wg, wu, wd, gpost, tm):
    T = x1.shape[0]
    tile = pl.BlockSpec((tm, D_MODEL), lambda i: (i, 0))
    full = lambda a: pl.BlockSpec(a.shape, lambda i: (0, 0))
    return pl.pallas_call(
        _ffn_kernel,
        grid=(T // tm,),
        in_specs=[tile, full(gpre), full(wg), full(wu), full(wd), full(gpost)],
        out_specs=tile,
        out_shape=jax.ShapeDtypeStruct((T, D_MODEL), F32),
        scratch_shapes=[pltpu.VMEM((tm, D_MODEL), BF16), pltpu.VMEM((tm, D_MODEL), F32)],
        compiler_params=pltpu.CompilerParams(
            dimension_semantics=("parallel",), vmem_limit_bytes=VMEM_LIMIT),
        name="ffn",
    )(x1, gpre, wg, wu, wd, gpost)


def _layer(x, norm_mix_pre, w_in, b_gate, mu_rw, w0, w_up, a0, a_up, g_up, k_k, k_a, r_k,
           lnx_w, lnx_b, w_sb_out, w_rw_out, w_o, norm_mix_post, norm_ffn_pre,
           w_ffn_gate, w_ffn_up, w_ffn_down, norm_ffn_post):
    B, S, D = x.shape
    T = B * S
    NS = B * HEADS
    assert D == D_MODEL and S % SB_TILE == 0 and NS % LANES == 0
    tm = min(512, T)
    row = lambda a: a.reshape(1, -1)

    xf = x.reshape(T, D)
    w_in_b = w_in.astype(BF16)
    qkv, p_rw = _in_proj(xf, row(norm_mix_pre), w_in_b[:, :MIX_COLS], tm)

    o_sb = _sb_attn(qkv.reshape(B, S, SB_COLS))

    wwa = jnp.zeros((W_LORA + A_LORA, 2 * WIDTH), F32)
    wwa = wwa.at[:W_LORA, :WIDTH].set(w_up).at[W_LORA:, WIDTH:].set(a_up)
    rw6, g = _rw_prep(p_rw.reshape(B, S, RW_COLS), row(mu_rw), row(w0), row(a0), wwa, g_up,
                      row(k_k), row(k_a), min(256, S))

    rw6 = rw6.reshape(6, B, S, HEADS, HEAD_DIM).transpose(0, 2, 4, 1, 3).reshape(
        6, S, HEAD_DIM, NS)
    per_stream = lambda a: jnp.tile(a.reshape(HEADS, HEAD_DIM).T, (1, B))
    y = _wkv(rw6, per_stream(lnx_w), per_stream(lnx_b), per_stream(r_k), min(64, S))
    y = y.reshape(S, HEAD_DIM, B, HEADS).transpose(2, 0, 3, 1).reshape(T, WIDTH)

    x1 = _merge(xf, o_sb.reshape(T, WIDTH), y, g.reshape(T, WIDTH), row(norm_mix_pre),
                w_in_b[:, MIX_COLS:], row(b_gate), w_sb_out.astype(BF16),
                w_rw_out.astype(BF16), w_o.astype(BF16), row(norm_mix_post), tm)
    out = _ffn(x1, row(norm_ffn_pre), w_ffn_gate.astype(BF16), w_ffn_up.astype(BF16),
               w_ffn_down.astype(BF16), row(norm_ffn_post), tm)
    return out.reshape(B, S, D)


def kernel(x, norm_mix_pre, w_in, b_gate, mu_rw, w0, w_up, a0, a_up, g_up, k_k, k_a, r_k,
           lnx_w, lnx_b, w_sb_out, w_rw_out, w_o, norm_mix_post, norm_ffn_pre,
           w_ffn_gate, w_ffn_up, w_ffn_down, norm_ffn_post):
    params = (norm_mix_pre, w_in, b_gate, mu_rw, w0, w_up, a0, a_up, g_up, k_k, k_a, r_k,
              lnx_w, lnx_b, w_sb_out, w_rw_out, w_o, norm_mix_post, norm_ffn_pre,
              w_ffn_gate, w_ffn_up, w_ffn_down, norm_ffn_post)
    for l in range(norm_mix_pre.shape[0]):
        x = _layer(x, *(p[l] for p in params))
    return x
```

```python
import functools

import jax
import jax.numpy as jnp
from jax import lax
from jax.experimental import pallas as pl
from jax.experimental.pallas import tpu as pltpu

F32 = jnp.float32
BF16 = jnp.bfloat16

D_MODEL = 1024
HEADS = 8
HEAD_DIM = 64
WIDTH = HEADS * HEAD_DIM
W_LORA = 64
A_LORA = 64
G_LORA = 128
SB_COLS = 3 * WIDTH
RW_COLS = 3 * WIDTH + W_LORA + A_LORA + G_LORA
MIX_COLS = SB_COLS + RW_COLS
D_FF = 2816
RW_STREAMS = 5
RMS_EPS = 1e-6
GN_EPS = HEAD_DIM * 1e-5
SB_TILE = 256
SB_SLOTS = 3
LOG2E = 1.4426950408889634
LANES = 128
SUBLANES = 8
MXU_N = 256
VMEM_LIMIT = 56 * 1024 * 1024


def _rms(x, g):
    ms = jnp.mean(x * x, axis=-1, keepdims=True)
    return x * lax.rsqrt(ms + RMS_EPS) * g


def _sigmoid(x):
    return 1.0 / (1.0 + jnp.exp(-x))


def _softplus(x):
    return jnp.maximum(x, 0.0) + jnp.log(1.0 + jnp.exp(-jnp.abs(x)))


def _in_proj_kernel(x_ref, g_ref, w_ref, qkv_ref, prw_ref, h_scr):
    h_scr[...] = _rms(x_ref[...], g_ref[...]).astype(BF16)
    for c0 in range(0, MIX_COLS, MXU_N):
        acc = jnp.dot(h_scr[...], w_ref[:, c0:c0 + MXU_N], preferred_element_type=F32)
        if c0 < WIDTH:
            acc = acc * (LOG2E * HEAD_DIM ** -0.5)
        if c0 < SB_COLS:
            qkv_ref[:, c0:c0 + MXU_N] = acc.astype(BF16)
        else:
            prw_ref[:, c0 - SB_COLS:c0 - SB_COLS + MXU_N] = acc


def _in_proj(xf, g, w, tm):
    T = xf.shape[0]
    return pl.pallas_call(
        _in_proj_kernel,
        grid=(T // tm,),
        in_specs=[
            pl.BlockSpec((tm, D_MODEL), lambda i: (i, 0)),
            pl.BlockSpec((1, D_MODEL), lambda i: (0, 0)),
            pl.BlockSpec((D_MODEL, MIX_COLS), lambda i: (0, 0)),
        ],
        out_specs=[
            pl.BlockSpec((tm, SB_COLS), lambda i: (i, 0)),
            pl.BlockSpec((tm, RW_COLS), lambda i: (i, 0)),
        ],
        out_shape=[
            jax.ShapeDtypeStruct((T, SB_COLS), BF16),
            jax.ShapeDtypeStruct((T, RW_COLS), F32),
        ],
        scratch_shapes=[pltpu.VMEM((tm, D_MODEL), BF16)],
        compiler_params=pltpu.CompilerParams(
            dimension_semantics=("parallel",), vmem_limit_bytes=VMEM_LIMIT),
        name="in_proj",
    )(xf, g, w)


def _sb_kernel(ti_ref, tj_ref, q_ref, k_ref, v_ref, o_ref, z_scr, c_scr, acc_scr, car_scr):
    S = q_ref.shape[1]
    nq = S // SB_TILE
    n_off = nq * (nq - 1) // 2
    row = lax.broadcasted_iota(jnp.int32, (SB_TILE, SB_TILE), 0)
    col = lax.broadcasted_iota(jnp.int32, (SB_TILE, SB_TILE), 1)
    incl_mat = jnp.where(row >= col, 1.0, 0.0).astype(BF16)
    strict = col < row
    head0 = lax.broadcasted_iota(jnp.int32, (SB_TILE, LANES), 1) < HEAD_DIM

    def rows(ref, t):
        return ref[0, pl.ds(pl.multiple_of(t * SB_TILE, SB_TILE), SB_TILE), :]

    def issue_logits(i, j, slot):
        q = rows(q_ref, i)
        kb = rows(k_ref, j)
        for h in range(2):
            qh = jnp.where(head0 if h == 0 else ~head0, q, jnp.zeros_like(q))
            z_scr[slot, h] = lax.dot_general(qh, kb, (((1,), (1,)), ((), ())),
                                             preferred_element_type=F32)

    def cumulate(slot, diag):
        for h in range(2):
            z = z_scr[slot, h]
            s = jnp.maximum(z, 0.0) + jnp.log(1.0 + jnp.exp2(-jnp.abs(z))) * LOG2E
            if diag:
                s = jnp.where(strict, s, 0.0)
            hi = s.astype(BF16)
            lo = (s - hi.astype(F32)).astype(BF16)
            c_scr[slot, h] = (jnp.dot(hi, incl_mat, preferred_element_type=F32)
                              + jnp.dot(lo, incl_mat, preferred_element_type=F32))

    def finish(i, j, slot, diag):
        vb = rows(v_ref, j)
        for h in range(2):
            c = c_scr[slot, h]
            e = z_scr[slot, h] - c
            if not diag:
                car = car_scr[i, h]
                e = e - jnp.concatenate([car, car], axis=1)
            w = jnp.exp2(e)
            if diag:
                w = jnp.where(strict, w, 0.0)
            pv = jnp.dot(w.astype(BF16), vb, preferred_element_type=F32)
            tot = jnp.broadcast_to(c[:, 0:1], (SB_TILE, LANES))
            if diag:
                acc_scr[i, h] = pv
                car_scr[i, h] = tot
            else:
                acc_scr[i, h] += pv
                car_scr[i, h] += tot

    def run(steps, tile_of, diag):
        def step(n, slot):
            issue_logits(*tile_of(jnp.minimum(n + 1, steps - 1)), (slot + 1) % SB_SLOTS)
            finish(*tile_of(n - 1), (slot - 1) % SB_SLOTS, diag)
            cumulate(slot, diag)

        issue_logits(*tile_of(0), 0)
        issue_logits(*tile_of(min(1, steps - 1)), 1)
        cumulate(0, diag)
        groups = (steps - 1) // SB_SLOTS

        def body(g, _):
            for u in range(SB_SLOTS):
                step(1 + SB_SLOTS * g + u, (1 + u) % SB_SLOTS)
            return 0

        lax.fori_loop(0, groups, body, 0)
        for n in range(1 + SB_SLOTS * groups, steps):
            step(n, n % SB_SLOTS)
        finish(*tile_of(steps - 1), (steps - 1) % SB_SLOTS, diag)

    run(nq, lambda n: (n, n), True)
    if n_off:
        run(n_off, lambda n: (ti_ref[n], tj_ref[n]), False)
    for i in range(nq):
        o_ref[0, i * SB_TILE:(i + 1) * SB_TILE, :] = jnp.where(
            head0, acc_scr[i, 0], acc_scr[i, 1]).astype(o_ref.dtype)


def _sb_attn(qkv):
    B, S, _ = qkv.shape
    npair = WIDTH // LANES
    nq = S // SB_TILE
    pairs = [(i, j) for i in range(nq) for j in range(i - 1, -1, -1)] or [(0, 0)]
    ti = jnp.array([p[0] for p in pairs], jnp.int32)
    tj = jnp.array([p[1] for p in pairs], jnp.int32)
    return pl.pallas_call(
        _sb_kernel,
        grid_spec=pltpu.PrefetchScalarGridSpec(
            num_scalar_prefetch=2,
            grid=(B, npair),
            in_specs=[
                pl.BlockSpec((1, S, LANES), lambda b, p, ti, tj: (b, 0, p)),
                pl.BlockSpec((1, S, LANES), lambda b, p, ti, tj: (b, 0, npair + p)),
                pl.BlockSpec((1, S, LANES), lambda b, p, ti, tj: (b, 0, 2 * npair + p)),
            ],
            out_specs=pl.BlockSpec((1, S, LANES), lambda b, p, ti, tj: (b, 0, p)),
            scratch_shapes=[
                pltpu.VMEM((SB_SLOTS, 2, SB_TILE, SB_TILE), F32),
                pltpu.VMEM((SB_SLOTS, 2, SB_TILE, SB_TILE), F32),
                pltpu.VMEM((nq, 2, SB_TILE, LANES), F32),
                pltpu.VMEM((nq, 2, SB_TILE, LANES), F32),
            ],
        ),
        out_shape=jax.ShapeDtypeStruct((B, S, WIDTH), BF16),
        compiler_params=pltpu.CompilerParams(
            dimension_semantics=("parallel", "parallel"), vmem_limit_bytes=VMEM_LIMIT),
        name="sb_attn",
    )(ti, tj, qkv, qkv, qkv)


def _rw_prep_kernel(p_ref, pp_ref, mu_ref, w0_ref, a0_ref, wwa_ref, gup_ref, out_ref, g_ref):
    i = pl.program_id(1)
    cur = p_ref[0]
    ts = cur.shape[0]
    last_prev = jnp.where(i > 0, pp_ref[0, SUBLANES - 1:SUBLANES, :], 0.0)
    rowid = lax.broadcasted_iota(jnp.int32, cur.shape, 0)
    prev = jnp.where(rowid == 0, last_prev, pltpu.roll(cur, 1, axis=0))
    p = cur + (prev - cur) * mu_ref[...]
    r = p[:, 0:WIDTH]
    k = p[:, WIDTH:2 * WIDTH]
    v = p[:, 2 * WIDTH:3 * WIDTH]
    xwa = p[:, 3 * WIDTH:3 * WIDTH + W_LORA + A_LORA]
    xg = p[:, 3 * WIDTH + W_LORA + A_LORA:]
    lane = lax.broadcasted_iota(jnp.int32, (ts, W_LORA + A_LORA), 1)
    twa = jnp.where(lane < W_LORA, jnp.tanh(xwa), xwa)
    wa = jnp.dot(twa, wwa_ref[...], preferred_element_type=F32, precision=lax.Precision.HIGHEST)
    w_raw = w0_ref[...] + wa[:, :WIDTH]
    decay = jnp.exp(-jnp.exp(-_softplus(-w_raw) - 0.5))
    a = _sigmoid(a0_ref[...] + wa[:, WIDTH:])
    g = jnp.dot(_sigmoid(xg), gup_ref[...], preferred_element_type=F32,
                precision=lax.Precision.HIGHEST)
    out_ref[0] = r
    out_ref[1] = decay
    out_ref[2] = k
    out_ref[3] = v
    out_ref[4] = a
    g_ref[0] = g.astype(g_ref.dtype)


def _rw_prep(p_rw, mu, w0, a0, wwa, g_up, ts):
    B, S, _ = p_rw.shape
    row = lambda c: pl.BlockSpec((1, c), lambda b, i: (0, 0))
    return pl.pallas_call(
        _rw_prep_kernel,
        grid=(B, S // ts),
        in_specs=[
            pl.BlockSpec((1, ts, RW_COLS), lambda b, i: (b, i, 0)),
            pl.BlockSpec((1, SUBLANES, RW_COLS),
                         lambda b, i: (b, jnp.maximum(i * (ts // SUBLANES) - 1, 0), 0)),
            row(RW_COLS), row(WIDTH), row(WIDTH),
            pl.BlockSpec((W_LORA + A_LORA, 2 * WIDTH), lambda b, i: (0, 0)),
            pl.BlockSpec((G_LORA, WIDTH), lambda b, i: (0, 0)),
        ],
        out_specs=[
            pl.BlockSpec((RW_STREAMS, ts, WIDTH), lambda b, i: (0, i, b)),
            pl.BlockSpec((1, ts, WIDTH), lambda b, i: (b, i, 0)),
        ],
        out_shape=[
            jax.ShapeDtypeStruct((RW_STREAMS, S, B * WIDTH), F32),
            jax.ShapeDtypeStruct((B, S, WIDTH), BF16),
        ],
        compiler_params=pltpu.CompilerParams(
            dimension_semantics=("parallel", "parallel"), vmem_limit_bytes=VMEM_LIMIT),
        name="rw_prep",
    )(p_rw, p_rw, mu, w0, a0, wwa, g_up)


def _wkv_kernel(r_ref, w_ref, k_ref, v_ref, a_ref, lw_ref, lb_ref, rk_ref, kk_ref, ka_ref, o_ref,
                st_ref, y_scr, x_scr, o_scr):
    @pl.when(pl.program_id(1) == 0)
    def _():
        st_ref[...] = jnp.zeros_like(st_ref)

    tc = r_ref.shape[0]
    npairs = tc // 2
    low = lax.broadcasted_iota(jnp.int32, (HEAD_DIM, LANES), 1) < HEAD_DIM
    swap = lambda x: pltpu.roll(x, HEAD_DIM, axis=1)
    sub = lax.broadcasted_iota(jnp.int32, (SUBLANES, LANES), 0)

    def fold(a, b, sh):
        first = (sub & sh) == 0
        if 2 * sh == SUBLANES:
            return jnp.where(first, a, b) + pltpu.roll(jnp.where(first, b, a), sh, axis=0)
        return jnp.where(first, a + pltpu.roll(a, SUBLANES - sh, axis=0),
                         b + pltpu.roll(b, sh, axis=0))

    def to_streams(p):
        t0 = pl.multiple_of(2 * p, 2)
        for n, ref in enumerate((r_ref, w_ref, k_ref, v_ref, a_ref)):
            xt = ref[pl.ds(t0, 2)].reshape(LANES, LANES).T
            x_scr[n, t0] = jnp.where(low, xt[:HEAD_DIM], swap(xt[HEAD_DIM:]))
            x_scr[n, t0 + 1] = jnp.where(low, swap(xt[:HEAD_DIM]), xt[HEAD_DIM:])

    def to_tokens(p):
        t0 = pl.multiple_of(2 * p, 2)
        ya = o_scr[t0]
        yb = o_scr[t0 + 1]
        m = jnp.concatenate([jnp.where(low, ya, swap(yb)), jnp.where(low, swap(ya), yb)], axis=0)
        o_ref[pl.ds(t0, 2)] = m.T.reshape(2, HEAD_DIM, LANES)

    def step(t):
        a = x_scr[4, t]
        k = x_scr[2, t]
        kk = k * kk_ref[...]
        k = k * (1.0 + (a - 1.0) * ka_ref[...])
        norm = jnp.sqrt(jnp.sum(kk * kk, axis=0, keepdims=True))
        kkn = kk * (1.0 / jnp.maximum(norm, 1e-12))
        nkk = -kkn
        bb = kkn * a
        w = x_scr[1, t]
        r = x_scr[0, t]
        vt = x_scr[3, t]
        for vb in range(HEAD_DIM // SUBLANES):
            part = []
            for j in range(SUBLANES):
                vi = vb * SUBLANES + j
                s = st_ref[vi]
                sa = jnp.sum(s * nkk, axis=0, keepdims=True)
                s = s * w + sa * bb + vt[vi:vi + 1] * k
                st_ref[vi] = s
                part.append(jnp.sum((s * r).reshape(SUBLANES, SUBLANES, LANES), axis=0))
            y_scr[vb * SUBLANES:(vb + 1) * SUBLANES, :] = fold(
                fold(fold(part[0], part[4], 4), fold(part[2], part[6], 4), 2),
                fold(fold(part[1], part[5], 4), fold(part[3], part[7], 4), 2), 1)
        y = y_scr[...]
        d = y - jnp.mean(y, axis=0, keepdims=True)
        var = jnp.mean(d * d, axis=0, keepdims=True)
        bonus = jnp.sum(r * k * rk_ref[...], axis=0, keepdims=True) * vt
        o_scr[t] = d * lax.rsqrt(var + GN_EPS) * lw_ref[...] + lb_ref[...] + bonus

    to_streams(0)
    step(0)
    step(1)
    to_streams(min(1, npairs - 1))

    def body(p, _):
        to_tokens(p - 1)
        step(2 * p)
        step(2 * p + 1)
        to_streams(jnp.minimum(p + 1, npairs - 1))
        return 0

    lax.fori_loop(1, npairs, body, 0)
    to_tokens(npairs - 1)


def _wkv(rw, lw, lb, rk, kkc, kac, tc):
    _, S, R, _ = rw.shape
    N = HEAD_DIM
    inp = lambda n: pl.BlockSpec((None, tc, N, LANES), lambda g, c: (n, c, g, 0))
    const = pl.BlockSpec((N, LANES), lambda g, c: (0, 0))
    return pl.pallas_call(
        _wkv_kernel,
        grid=(R // N, S // tc),
        in_specs=[inp(n) for n in range(RW_STREAMS)] + [const] * 5,
        out_specs=pl.BlockSpec((tc, N, LANES), lambda g, c: (c, g, 0)),
        out_shape=jax.ShapeDtypeStruct((S, R, LANES), F32),
        scratch_shapes=[pltpu.VMEM((N, N, LANES), F32), pltpu.VMEM((N, LANES), F32),
                        pltpu.VMEM((RW_STREAMS, tc, N, LANES), F32),
                        pltpu.VMEM((tc, N, LANES), F32)],
        compiler_params=pltpu.CompilerParams(
            dimension_semantics=("parallel", "arbitrary"), vmem_limit_bytes=VMEM_LIMIT),
        name="wkv",
    )(*([rw] * RW_STREAMS), lw, lb, rk, kkc, kac)


def _merge_kernel(x_ref, osb_ref, y_ref, g_ref, gpre_ref, wg_ref, bg_ref, wsb_ref, wrw_ref,
                  wo_ref, gpost_ref, x1_ref, h_scr, orw_scr, m_scr, f_scr):
    x = x_ref[...]
    h_scr[...] = _rms(x, gpre_ref[...]).astype(BF16)
    orw_scr[...] = (y_ref[...] * g_ref[...].astype(F32)).astype(BF16)
    for c0 in range(0, D_MODEL, MXU_N):
        cs = slice(c0, c0 + MXU_N)
        cr = slice(D_MODEL + c0, D_MODEL + c0 + MXU_N)
        g_sb = _sigmoid(jnp.dot(h_scr[...], wg_ref[:, cs], preferred_element_type=F32)
                        + bg_ref[:, cs])
        g_rw = _sigmoid(jnp.dot(h_scr[...], wg_ref[:, cr], preferred_element_type=F32)
                        + bg_ref[:, cr])
        a_sb = jnp.dot(osb_ref[...], wsb_ref[:, cs], preferred_element_type=F32)
        a_rw = jnp.dot(orw_scr[...], wrw_ref[:, cs], preferred_element_type=F32)
        m_scr[:, cs] = (g_sb * a_sb + g_rw * a_rw).astype(BF16)
    for c0 in range(0, D_MODEL, MXU_N):
        cs = slice(c0, c0 + MXU_N)
        f_scr[:, cs] = jnp.dot(m_scr[...], wo_ref[:, cs], preferred_element_type=F32)
    x1_ref[...] = x + _rms(f_scr[...], gpost_ref[...])


def _merge(xf, o_sb, y, g, gpre, wg, bg, wsb, wrw, wo, gpost, tm):
    T = xf.shape[0]
    spb = y.shape[0] // tm
    tile = lambda c: pl.BlockSpec((tm, c), lambda i: (i, 0))
    y_tile = pl.BlockSpec((tm, WIDTH), lambda i: (i % spb, i // spb))
    full = lambda a: pl.BlockSpec(a.shape, lambda i: (0, 0))
    return pl.pallas_call(
        _merge_kernel,
        grid=(T // tm,),
        in_specs=[tile(D_MODEL), tile(WIDTH), y_tile, tile(WIDTH), full(gpre), full(wg),
                  full(bg), full(wsb), full(wrw), full(wo), full(gpost)],
        out_specs=tile(D_MODEL),
        out_shape=jax.ShapeDtypeStruct((T, D_MODEL), F32),
        scratch_shapes=[pltpu.VMEM((tm, D_MODEL), BF16), pltpu.VMEM((tm, WIDTH), BF16),
                        pltpu.VMEM((tm, D_MODEL), BF16), pltpu.VMEM((tm, D_MODEL), F32)],
        compiler_params=pltpu.CompilerParams(
            dimension_semantics=("parallel",), vmem_limit_bytes=VMEM_LIMIT),
        name="merge",
    )(xf, o_sb, y, g, gpre, wg, bg, wsb, wrw, wo, gpost)


def _ffn_kernel(x_ref, gpre_ref, wg_ref, wu_ref, wd_ref, gpost_ref, o_ref, h_scr, acc_scr):
    x = x_ref[...]
    h_scr[...] = _rms(x, gpre_ref[...]).astype(BF16)
    for c0 in range(0, D_FF, MXU_N):
        cs = slice(c0, c0 + MXU_N)
        gt = jnp.dot(h_scr[...], wg_ref[:, cs], preferred_element_type=F32)
        up = jnp.dot(h_scr[...], wu_ref[:, cs], preferred_element_type=F32)
        act = (gt * _sigmoid(gt) * up).astype(BF16)
        part = jnp.dot(act, wd_ref[cs, :], preferred_element_type=F32)
        if c0 == 0:
            acc_scr[...] = part
        else:
            acc_scr[...] += part
    o_ref[...] = x + _rms(acc_scr[...], gpost_ref[...])


def _ffn(x1, gpre, wg, wu, wd, gpost, tm):
    T = x1.shape[0]
    tile = pl.BlockSpec((tm, D_MODEL), lambda i: (i, 0))
    full = lambda a: pl.BlockSpec(a.shape, lambda i: (0, 0))
    return pl.pallas_call(
        _ffn_kernel,
        grid=(T // tm,),
        in_specs=[tile, full(gpre), full(wg), full(wu), full(wd), full(gpost)],
        out_specs=tile,
        out_shape=jax.ShapeDtypeStruct((T, D_MODEL), F32),
        scratch_shapes=[pltpu.VMEM((tm, D_MODEL), BF16), pltpu.VMEM((tm, D_MODEL), F32)],
        compiler_params=pltpu.CompilerParams(
            dimension_semantics=("parallel",), vmem_limit_bytes=VMEM_LIMIT),
        name="ffn",
    )(x1, gpre, wg, wu, wd, gpost)


def _layer(x, norm_mix_pre, w_in, b_gate, mu_rw, w0, w_up, a0, a_up, g_up, k_k, k_a, r_k,
           lnx_w, lnx_b, w_sb_out, w_rw_out, w_o, norm_mix_post, norm_ffn_pre,
           w_ffn_gate, w_ffn_up, w_ffn_down, norm_ffn_post):
    B, S, D = x.shape
    T = B * S
    NS = B * HEADS
    assert D == D_MODEL and S % SB_TILE == 0 and NS % LANES == 0
    tm = min(512, S)
    row = lambda a: a.reshape(1, -1)

    xf = x.reshape(T, D)
    w_in_b = w_in.astype(BF16)
    qkv, p_rw = _in_proj(xf, row(norm_mix_pre), w_in_b[:, :MIX_COLS], tm)

    o_sb = _sb_attn(qkv.reshape(B, S, SB_COLS))

    wwa = jnp.zeros((W_LORA + A_LORA, 2 * WIDTH), F32)
    wwa = wwa.at[:W_LORA, :WIDTH].set(w_up).at[W_LORA:, WIDTH:].set(a_up)
    rw, g = _rw_prep(p_rw.reshape(B, S, RW_COLS), row(mu_rw), row(w0), row(a0), wwa, g_up,
                     min(256, S))

    lane_head = 2 * (jnp.arange(LANES) % (HEADS // 2)) + jnp.arange(LANES) // HEAD_DIM
    per_stream = lambda a: a.reshape(HEADS, HEAD_DIM)[lane_head].T
    y = _wkv(rw.reshape(RW_STREAMS, S, B * WIDTH // LANES, LANES), per_stream(lnx_w),
             per_stream(lnx_b), per_stream(r_k), per_stream(k_k), per_stream(k_a), min(64, S))
    y = y.reshape(S, B * WIDTH)

    x1 = _merge(xf, o_sb.reshape(T, WIDTH), y, g.reshape(T, WIDTH), row(norm_mix_pre),
                w_in_b[:, MIX_COLS:], row(b_gate), w_sb_out.astype(BF16),
                w_rw_out.astype(BF16), w_o.astype(BF16), row(norm_mix_post), tm)
    out = _ffn(x1, row(norm_ffn_pre), w_ffn_gate.astype(BF16), w_ffn_up.astype(BF16),
               w_ffn_down.astype(BF16), row(norm_ffn_post), tm)
    return out.reshape(B, S, D)


def kernel(x, norm_mix_pre, w_in, b_gate, mu_rw, w0, w_up, a0, a_up, g_up, k_k, k_a, r_k,
           lnx_w, lnx_b, w_sb_out, w_rw_out, w_o, norm_mix_post, norm_ffn_pre,
           w_ffn_gate, w_ffn_up, w_ffn_down, norm_ffn_post):
    params = (norm_mix_pre, w_in, b_gate, mu_rw, w0, w_up, a0, a_up, g_up, k_k, k_a, r_k,
              lnx_w, lnx_b, w_sb_out, w_rw_out, w_o, norm_mix_post, norm_ffn_pre,
              w_ffn_gate, w_ffn_up, w_ffn_down, norm_ffn_post)
    for l in range(norm_mix_pre.shape[0]):
        x = _layer(x, *(p[l] for p in params))
    return x
```

```python
import functools

import jax
import jax.numpy as jnp
from jax import lax
from jax.experimental import pallas as pl
from jax.experimental.pallas import tpu as pltpu

F32 = jnp.float32
BF16 = jnp.bfloat16

D_MODEL = 1024
HEADS = 8
HEAD_DIM = 64
WIDTH = HEADS * HEAD_DIM
W_LORA = 64
A_LORA = 64
G_LORA = 128
SB_COLS = 3 * WIDTH
RW_COLS = 3 * WIDTH + W_LORA + A_LORA + G_LORA
MIX_COLS = SB_COLS + RW_COLS
D_FF = 2816
RW_STREAMS = 5
RW_ROWS = 256
RMS_EPS = 1e-6
GN_EPS = HEAD_DIM * 1e-5
SB_TILE = 256
SB_SLOTS = 3
LOG2E = 1.4426950408889634
LANES = 128
SUBLANES = 8
MXU_N = 256
VMEM_LIMIT = 56 * 1024 * 1024


def _rms(x, g):
    ms = jnp.mean(x * x, axis=-1, keepdims=True)
    return x * lax.rsqrt(ms + RMS_EPS) * g


def _sigmoid(x):
    return 1.0 / (1.0 + jnp.exp(-x))


def _softplus(x):
    return jnp.maximum(x, 0.0) + jnp.log(1.0 + jnp.exp(-jnp.abs(x)))


def _dot3(x, w_hi, w_lo):
    x_hi = x.astype(BF16)
    x_lo = (x - x_hi.astype(F32)).astype(BF16)
    return (jnp.dot(x_hi, w_hi, preferred_element_type=F32)
            + jnp.dot(x_lo, w_hi, preferred_element_type=F32)
            + jnp.dot(x_hi, w_lo, preferred_element_type=F32))


def _in_proj_kernel(tiles_per_seq, x_ref, g_ref, w_ref, mu_ref, w0_ref, a0_ref, wwa_hi_ref,
                    wwa_lo_ref, gup_hi_ref, gup_lo_ref, qkv_ref, rw_ref, gate_ref,
                    h_scr, p_scr, last_scr):
    tm = x_ref.shape[0]
    h_scr[...] = _rms(x_ref[...], g_ref[...]).astype(BF16)
    for c0 in range(0, MIX_COLS, MXU_N):
        acc = jnp.dot(h_scr[...], w_ref[:, c0:c0 + MXU_N], preferred_element_type=F32)
        if c0 < WIDTH:
            acc = acc * (LOG2E * HEAD_DIM ** -0.5)
        if c0 < SB_COLS:
            qkv_ref[:, c0:c0 + MXU_N] = acc.astype(BF16)
        else:
            p_scr[:, c0 - SB_COLS:c0 - SB_COLS + MXU_N] = acc

    seq_start = pl.program_id(0) % tiles_per_seq == 0
    rowid = lax.broadcasted_iota(jnp.int32, (RW_ROWS, RW_COLS), 0)
    lane = lax.broadcasted_iota(jnp.int32, (RW_ROWS, W_LORA + A_LORA), 1)
    for r0 in range(0, tm, RW_ROWS):
        cur = p_scr[r0:r0 + RW_ROWS]
        if r0 == 0:
            before = jnp.where(seq_start, 0.0, last_scr[SUBLANES - 1:SUBLANES])
        else:
            before = p_scr[r0 - 1:r0]
        prev = jnp.where(rowid == 0, before, pltpu.roll(cur, 1, axis=0))
        p = cur + (prev - cur) * mu_ref[...]
        k = p[:, WIDTH:2 * WIDTH]
        xwa = p[:, 3 * WIDTH:3 * WIDTH + W_LORA + A_LORA]
        xg = p[:, 3 * WIDTH + W_LORA + A_LORA:]
        twa = jnp.where(lane < W_LORA, jnp.tanh(xwa), xwa)
        wa = _dot3(twa, wwa_hi_ref[...], wwa_lo_ref[...])
        w_raw = w0_ref[...] + wa[:, :WIDTH]
        rs = slice(r0, r0 + RW_ROWS)
        rw_ref[0, rs] = p[:, 0:WIDTH]
        rw_ref[1, rs] = -jnp.exp(-_softplus(-w_raw) - 0.5)
        rw_ref[2, rs] = k
        rw_ref[3, rs] = p[:, 2 * WIDTH:3 * WIDTH]
        rw_ref[4, rs] = _sigmoid(a0_ref[...] + wa[:, WIDTH:])
        gate_ref[rs] = _dot3(_sigmoid(xg), gup_hi_ref[...], gup_lo_ref[...]).astype(BF16)
    last_scr[...] = p_scr[tm - SUBLANES:tm]


def _in_proj(xf, g, w, mu, w0, a0, wwa, g_up, seq_len, tm):
    T = xf.shape[0]
    spb = seq_len // tm
    hi = lambda a: a.astype(BF16)
    lo = lambda a: (a - a.astype(BF16).astype(F32)).astype(BF16)
    full = lambda a: pl.BlockSpec(a.shape, lambda i: (0, 0))
    consts = (g, w, mu, w0, a0, hi(wwa), lo(wwa), hi(g_up), lo(g_up))
    return pl.pallas_call(
        functools.partial(_in_proj_kernel, spb),
        grid=(T // tm,),
        in_specs=[pl.BlockSpec((tm, D_MODEL), lambda i: (i, 0))] + [full(a) for a in consts],
        out_specs=[
            pl.BlockSpec((tm, SB_COLS), lambda i: (i, 0)),
            pl.BlockSpec((RW_STREAMS, tm, WIDTH), lambda i: (0, i % spb, i // spb)),
            pl.BlockSpec((tm, WIDTH), lambda i: (i, 0)),
        ],
        out_shape=[
            jax.ShapeDtypeStruct((T, SB_COLS), BF16),
            jax.ShapeDtypeStruct((RW_STREAMS, seq_len, (T // seq_len) * WIDTH), F32),
            jax.ShapeDtypeStruct((T, WIDTH), BF16),
        ],
        scratch_shapes=[pltpu.VMEM((tm, D_MODEL), BF16), pltpu.VMEM((tm, RW_COLS), F32),
                        pltpu.VMEM((SUBLANES, RW_COLS), F32)],
        compiler_params=pltpu.CompilerParams(
            dimension_semantics=("arbitrary",), vmem_limit_bytes=VMEM_LIMIT),
        name="in_proj",
    )(xf, *consts)


def _sb_kernel(ti_ref, tj_ref, q_ref, k_ref, v_ref, o_ref, z_scr, c_scr, acc_scr, car_scr):
    S = q_ref.shape[1]
    nq = S // SB_TILE
    n_off = nq * (nq - 1) // 2
    row = lax.broadcasted_iota(jnp.int32, (SB_TILE, SB_TILE), 0)
    col = lax.broadcasted_iota(jnp.int32, (SB_TILE, SB_TILE), 1)
    incl_mat = jnp.where(row >= col, 1.0, 0.0).astype(BF16)
    incl_mat2 = jnp.concatenate([incl_mat, incl_mat], axis=0)
    strict = col < row
    head0 = lax.broadcasted_iota(jnp.int32, (SB_TILE, LANES), 1) < HEAD_DIM

    def rows(ref, t):
        return ref[0, pl.ds(pl.multiple_of(t * SB_TILE, SB_TILE), SB_TILE), :]

    def issue_logits(i, j, slot):
        q = rows(q_ref, i)
        kb = rows(k_ref, j)
        for h in range(2):
            qh = jnp.where(head0 if h == 0 else ~head0, q, jnp.zeros_like(q))
            z_scr[slot, h] = lax.dot_general(qh, kb, (((1,), (1,)), ((), ())),
                                             preferred_element_type=F32)

    def cumulate(slot, diag):
        for h in range(2):
            z = z_scr[slot, h]
            s = jnp.maximum(z, 0.0) + jnp.log(1.0 + jnp.exp2(-jnp.abs(z))) * LOG2E
            if diag:
                s = jnp.where(strict, s, 0.0)
            hi = s.astype(BF16)
            lo = (s - hi.astype(F32)).astype(BF16)
            c_scr[slot, h] = jnp.dot(jnp.concatenate([hi, lo], axis=1), incl_mat2,
                                     preferred_element_type=F32)

    def finish(i, j, slot, diag):
        vb = rows(v_ref, j)
        for h in range(2):
            c = c_scr[slot, h]
            e = z_scr[slot, h] - c
            if not diag:
                car = car_scr[i, h]
                e = e - jnp.concatenate([car, car], axis=1)
            w = jnp.exp2(e)
            if diag:
                w = jnp.where(strict, w, 0.0)
            pv = jnp.dot(w.astype(BF16), vb, preferred_element_type=F32)
            tot = jnp.broadcast_to(c[:, 0:1], (SB_TILE, LANES))
            if diag:
                acc_scr[i, h] = pv
                car_scr[i, h] = tot
            else:
                acc_scr[i, h] += pv
                car_scr[i, h] += tot

    def run(steps, tile_of, diag):
        def step(n, slot):
            issue_logits(*tile_of(jnp.minimum(n + 1, steps - 1)), (slot + 1) % SB_SLOTS)
            finish(*tile_of(n - 1), (slot - 1) % SB_SLOTS, diag)
            cumulate(slot, diag)

        issue_logits(*tile_of(0), 0)
        issue_logits(*tile_of(min(1, steps - 1)), 1)
        cumulate(0, diag)
        groups = (steps - 1) // SB_SLOTS

        def body(g, _):
            for u in range(SB_SLOTS):
                step(1 + SB_SLOTS * g + u, (1 + u) % SB_SLOTS)
            return 0

        lax.fori_loop(0, groups, body, 0)
        for n in range(1 + SB_SLOTS * groups, steps):
            step(n, n % SB_SLOTS)
        finish(*tile_of(steps - 1), (steps - 1) % SB_SLOTS, diag)

    run(nq, lambda n: (n, n), True)
    if n_off:
        run(n_off, lambda n: (ti_ref[n], tj_ref[n]), False)
    for i in range(nq):
        o_ref[0, i * SB_TILE:(i + 1) * SB_TILE, :] = jnp.where(
            head0, acc_scr[i, 0], acc_scr[i, 1]).astype(o_ref.dtype)


def _sb_attn(qkv):
    B, S, _ = qkv.shape
    npair = WIDTH // LANES
    nq = S // SB_TILE
    pairs = [(i, j) for i in range(nq) for j in range(i - 1, -1, -1)] or [(0, 0)]
    ti = jnp.array([p[0] for p in pairs], jnp.int32)
    tj = jnp.array([p[1] for p in pairs], jnp.int32)
    return pl.pallas_call(
        _sb_kernel,
        grid_spec=pltpu.PrefetchScalarGridSpec(
            num_scalar_prefetch=2,
            grid=(B, npair),
            in_specs=[
                pl.BlockSpec((1, S, LANES), lambda b, p, ti, tj: (b, 0, p)),
                pl.BlockSpec((1, S, LANES), lambda b, p, ti, tj: (b, 0, npair + p)),
                pl.BlockSpec((1, S, LANES), lambda b, p, ti, tj: (b, 0, 2 * npair + p)),
            ],
            out_specs=pl.BlockSpec((1, S, LANES), lambda b, p, ti, tj: (b, 0, p)),
            scratch_shapes=[
                pltpu.VMEM((SB_SLOTS, 2, SB_TILE, SB_TILE), F32),
                pltpu.VMEM((SB_SLOTS, 2, SB_TILE, SB_TILE), F32),
                pltpu.VMEM((nq, 2, SB_TILE, LANES), F32),
                pltpu.VMEM((nq, 2, SB_TILE, LANES), F32),
            ],
        ),
        out_shape=jax.ShapeDtypeStruct((B, S, WIDTH), BF16),
        compiler_params=pltpu.CompilerParams(
            dimension_semantics=("parallel", "parallel"), vmem_limit_bytes=VMEM_LIMIT),
        name="sb_attn",
    )(ti, tj, qkv, qkv, qkv)


def _wkv_kernel(r_ref, w_ref, k_ref, v_ref, a_ref, lw_ref, lb_ref, rk_ref, kk_ref, ka_ref, o_ref,
                st_ref, y_scr, x_scr, o_scr, ld_scr):
    @pl.when(pl.program_id(1) == 0)
    def _():
        st_ref[...] = jnp.zeros_like(st_ref)

    ld_scr[...] = jnp.zeros_like(ld_scr)
    tc = r_ref.shape[0]
    npairs = tc // 2
    low = lax.broadcasted_iota(jnp.int32, (HEAD_DIM, LANES), 1) < HEAD_DIM
    swap = lambda x: pltpu.roll(x, HEAD_DIM, axis=1)
    sub = lax.broadcasted_iota(jnp.int32, (SUBLANES, LANES), 0)

    def fold(a, b, sh):
        first = (sub & sh) == 0
        if 2 * sh == SUBLANES:
            return jnp.where(first, a, b) + pltpu.roll(jnp.where(first, b, a), sh, axis=0)
        return jnp.where(first, a + pltpu.roll(a, SUBLANES - sh, axis=0),
                         b + pltpu.roll(b, sh, axis=0))

    def to_streams(p):
        t0 = pl.multiple_of(2 * p, 2)
        for n, ref in enumerate((r_ref, w_ref, k_ref, v_ref, a_ref)):
            xt = ref[pl.ds(t0, 2)].reshape(LANES, LANES).T
            x_scr[n, t0] = jnp.where(low, xt[:HEAD_DIM], swap(xt[HEAD_DIM:]))
            x_scr[n, t0 + 1] = jnp.where(low, swap(xt[:HEAD_DIM]), xt[HEAD_DIM:])

    def to_tokens(p):
        t0 = pl.multiple_of(2 * p, 2)
        ya = o_scr[t0]
        yb = o_scr[t0 + 1]
        m = jnp.concatenate([jnp.where(low, ya, swap(yb)), jnp.where(low, swap(ya), yb)], axis=0)
        o_ref[pl.ds(t0, 2)] = m.T.reshape(2, HEAD_DIM, LANES)

    def step(t):
        a = x_scr[4, t]
        k = x_scr[2, t]
        kk = k * kk_ref[...]
        k = k * (1.0 + (a - 1.0) * ka_ref[...])
        norm = jnp.sqrt(jnp.sum(kk * kk, axis=0, keepdims=True))
        kkn = kk * (1.0 / jnp.maximum(norm, 1e-12))
        r = x_scr[0, t]
        vt = x_scr[3, t]
        g_before = jnp.exp(ld_scr[...])
        ld = ld_scr[...] + x_scr[1, t]
        ld_scr[...] = ld
        g_now = jnp.exp(ld)
        g_inv = jnp.exp(-ld)
        nkk = -kkn * g_before
        bb = kkn * a * g_inv
        ks = k * g_inv
        rs = r * g_now
        for vb in range(HEAD_DIM // SUBLANES):
            part = []
            for j in range(SUBLANES):
                vi = vb * SUBLANES + j
                s = st_ref[vi]
                sa = jnp.sum(s * nkk, axis=0, keepdims=True)
                s = s + sa * bb + vt[vi:vi + 1] * ks
                st_ref[vi] = s
                part.append(jnp.sum((s * rs).reshape(SUBLANES, SUBLANES, LANES), axis=0))
            y_scr[vb * SUBLANES:(vb + 1) * SUBLANES, :] = fold(
                fold(fold(part[0], part[4], 4), fold(part[2], part[6], 4), 2),
                fold(fold(part[1], part[5], 4), fold(part[3], part[7], 4), 2), 1)
        y = y_scr[...]
        d = y - jnp.mean(y, axis=0, keepdims=True)
        var = jnp.mean(d * d, axis=0, keepdims=True)
        bonus = jnp.sum(r * k * rk_ref[...], axis=0, keepdims=True) * vt
        o_scr[t] = d * lax.rsqrt(var + GN_EPS) * lw_ref[...] + lb_ref[...] + bonus

    to_streams(0)
    step(0)
    step(1)
    to_streams(min(1, npairs - 1))

    def body(p, _):
        to_tokens(p - 1)
        step(2 * p)
        step(2 * p + 1)
        to_streams(jnp.minimum(p + 1, npairs - 1))
        return 0

    lax.fori_loop(1, npairs, body, 0)
    to_tokens(npairs - 1)
    g_end = jnp.exp(ld_scr[...])
    for vi in range(HEAD_DIM):
        st_ref[vi] = st_ref[vi] * g_end


def _wkv(rw, lw, lb, rk, kkc, kac, tc):
    _, S, R, _ = rw.shape
    N = HEAD_DIM
    inp = lambda n: pl.BlockSpec((None, tc, N, LANES), lambda g, c: (n, c, g, 0))
    const = pl.BlockSpec((N, LANES), lambda g, c: (0, 0))
    return pl.pallas_call(
        _wkv_kernel,
        grid=(R // N, S // tc),
        in_specs=[inp(n) for n in range(RW_STREAMS)] + [const] * 5,
        out_specs=pl.BlockSpec((tc, N, LANES), lambda g, c: (c, g, 0)),
        out_shape=jax.ShapeDtypeStruct((S, R, LANES), F32),
        scratch_shapes=[pltpu.VMEM((N, N, LANES), F32), pltpu.VMEM((N, LANES), F32),
                        pltpu.VMEM((RW_STREAMS, tc, N, LANES), F32),
                        pltpu.VMEM((tc, N, LANES), F32), pltpu.VMEM((N, LANES), F32)],
        compiler_params=pltpu.CompilerParams(
            dimension_semantics=("parallel", "arbitrary"), vmem_limit_bytes=VMEM_LIMIT),
        name="wkv",
    )(*([rw] * RW_STREAMS), lw, lb, rk, kkc, kac)


def _merge_kernel(x_ref, osb_ref, y_ref, g_ref, gpre_ref, wg_ref, bg_ref, wsb_ref, wrw_ref,
                  wo_ref, gpost_ref, x1_ref, h_scr, orw_scr, m_scr, f_scr):
    x = x_ref[...]
    h_scr[...] = _rms(x, gpre_ref[...]).astype(BF16)
    orw_scr[...] = (y_ref[...] * g_ref[...].astype(F32)).astype(BF16)
    for c0 in range(0, D_MODEL, MXU_N):
        cs = slice(c0, c0 + MXU_N)
        cr = slice(D_MODEL + c0, D_MODEL + c0 + MXU_N)
        g_sb = _sigmoid(jnp.dot(h_scr[...], wg_ref[:, cs], preferred_element_type=F32)
                        + bg_ref[:, cs])
        g_rw = _sigmoid(jnp.dot(h_scr[...], wg_ref[:, cr], preferred_element_type=F32)
                        + bg_ref[:, cr])
        a_sb = jnp.dot(osb_ref[...], wsb_ref[:, cs], preferred_element_type=F32)
        a_rw = jnp.dot(orw_scr[...], wrw_ref[:, cs], preferred_element_type=F32)
        m_scr[:, cs] = (g_sb * a_sb + g_rw * a_rw).astype(BF16)
    for c0 in range(0, D_MODEL, MXU_N):
        cs = slice(c0, c0 + MXU_N)
        f_scr[:, cs] = jnp.dot(m_scr[...], wo_ref[:, cs], preferred_element_type=F32)
    x1_ref[...] = x + _rms(f_scr[...], gpost_ref[...])


def _merge(xf, o_sb, y, g, gpre, wg, bg, wsb, wrw, wo, gpost, tm):
    T = xf.shape[0]
    spb = y.shape[0] // tm
    tile = lambda c: pl.BlockSpec((tm, c), lambda i: (i, 0))
    y_tile = pl.BlockSpec((tm, WIDTH), lambda i: (i % spb, i // spb))
    full = lambda a: pl.BlockSpec(a.shape, lambda i: (0, 0))
    return pl.pallas_call(
        _merge_kernel,
        grid=(T // tm,),
        in_specs=[tile(D_MODEL), tile(WIDTH), y_tile, tile(WIDTH), full(gpre), full(wg),
                  full(bg), full(wsb), full(wrw), full(wo), full(gpost)],
        out_specs=tile(D_MODEL),
        out_shape=jax.ShapeDtypeStruct((T, D_MODEL), F32),
        scratch_shapes=[pltpu.VMEM((tm, D_MODEL), BF16), pltpu.VMEM((tm, WIDTH), BF16),
                        pltpu.VMEM((tm, D_MODEL), BF16), pltpu.VMEM((tm, D_MODEL), F32)],
        compiler_params=pltpu.CompilerParams(
            dimension_semantics=("parallel",), vmem_limit_bytes=VMEM_LIMIT),
        name="merge",
    )(xf, o_sb, y, g, gpre, wg, bg, wsb, wrw, wo, gpost)


def _ffn_kernel(x_ref, gpre_ref, wg_ref, wu_ref, wd_ref, gpost_ref, o_ref, h_scr, acc_scr):
    x = x_ref[...]
    h_scr[...] = _rms(x, gpre_ref[...]).astype(BF16)
    for c0 in range(0, D_FF, MXU_N):
        cs = slice(c0, c0 + MXU_N)
        gt = jnp.dot(h_scr[...], wg_ref[:, cs], preferred_element_type=F32)
        up = jnp.dot(h_scr[...], wu_ref[:, cs], preferred_element_type=F32)
        act = (gt * _sigmoid(gt) * up).astype(BF16)
        part = jnp.dot(act, wd_ref[cs, :], preferred_element_type=F32)
        if c0 == 0:
            acc_scr[...] = part
        else:
            acc_scr[...] += part
    o_ref[...] = x + _rms(acc_scr[...], gpost_ref[...])


def _ffn(x1, gpre, wg, wu, wd, gpost, tm):
    T = x1.shape[0]
    tile = pl.BlockSpec((tm, D_MODEL), lambda i: (i, 0))
    full = lambda a: pl.BlockSpec(a.shape, lambda i: (0, 0))
    return pl.pallas_call(
        _ffn_kernel,
        grid=(T // tm,),
        in_specs=[tile, full(gpre), full(wg), full(wu), full(wd), full(gpost)],
        out_specs=tile,
        out_shape=jax.ShapeDtypeStruct((T, D_MODEL), F32),
        scratch_shapes=[pltpu.VMEM((tm, D_MODEL), BF16), pltpu.VMEM((tm, D_MODEL), F32)],
        compiler_params=pltpu.CompilerParams(
            dimension_semantics=("parallel",), vmem_limit_bytes=VMEM_LIMIT),
        name="ffn",
    )(x1, gpre, wg, wu, wd, gpost)


def _layer(x, norm_mix_pre, w_in, b_gate, mu_rw, w0, w_up, a0, a_up, g_up, k_k, k_a, r_k,
           lnx_w, lnx_b, w_sb_out, w_rw_out, w_o, norm_mix_post, norm_ffn_pre,
           w_ffn_gate, w_ffn_up, w_ffn_down, norm_ffn_post):
    B, S, D = x.shape
    T = B * S
    NS = B * HEADS
    assert D == D_MODEL and S % SB_TILE == 0 and NS % LANES == 0
    tm = min(512, S)
    row = lambda a: a.reshape(1, -1)

    xf = x.reshape(T, D)
    w_in_b = w_in.astype(BF16)
    wwa = jnp.zeros((W_LORA + A_LORA, 2 * WIDTH), F32)
    wwa = wwa.at[:W_LORA, :WIDTH].set(w_up).at[W_LORA:, WIDTH:].set(a_up)
    qkv, rw, g = _in_proj(xf, row(norm_mix_pre), w_in_b[:, :MIX_COLS], row(mu_rw), row(w0),
                          row(a0), wwa, g_up, S, tm)

    o_sb = _sb_attn(qkv.reshape(B, S, SB_COLS))

    lane_head = 2 * (jnp.arange(LANES) % (HEADS // 2)) + jnp.arange(LANES) // HEAD_DIM
    per_stream = lambda a: a.reshape(HEADS, HEAD_DIM)[lane_head].T
    y = _wkv(rw.reshape(RW_STREAMS, S, B * WIDTH // LANES, LANES), per_stream(lnx_w),
             per_stream(lnx_b), per_stream(r_k), per_stream(k_k), per_stream(k_a), min(64, S))
    y = y.reshape(S, B * WIDTH)

    x1 = _merge(xf, o_sb.reshape(T, WIDTH), y, g, row(norm_mix_pre),
                w_in_b[:, MIX_COLS:], row(b_gate), w_sb_out.astype(BF16),
                w_rw_out.astype(BF16), w_o.astype(BF16), row(norm_mix_post), tm)
    out = _ffn(x1, row(norm_ffn_pre), w_ffn_gate.astype(BF16), w_ffn_up.astype(BF16),
               w_ffn_down.astype(BF16), row(norm_ffn_post), tm)
    return out.reshape(B, S, D)


def kernel(x, norm_mix_pre, w_in, b_gate, mu_rw, w0, w_up, a0, a_up, g_up, k_k, k_a, r_k,
           lnx_w, lnx_b, w_sb_out, w_rw_out, w_o, norm_mix_post, norm_ffn_pre,
           w_ffn_gate, w_ffn_up, w_ffn_down, norm_ffn_post):
    params = (norm_mix_pre, w_in, b_gate, mu_rw, w0, w_up, a0, a_up, g_up, k_k, k_a, r_k,
              lnx_w, lnx_b, w_sb_out, w_rw_out, w_o, norm_mix_post, norm_ffn_pre,
              w_ffn_gate, w_ffn_up, w_ffn_down, norm_ffn_post)
    for l in range(norm_mix_pre.shape[0]):
        x = _layer(x, *(p[l] for p in params))
    return x
```

```python
import functools

import jax
import jax.numpy as jnp
from jax import lax
from jax.experimental import pallas as pl
from jax.experimental.pallas import tpu as pltpu

F32 = jnp.float32
BF16 = jnp.bfloat16

D_MODEL = 1024
HEADS = 8
HEAD_DIM = 64
WIDTH = HEADS * HEAD_DIM
W_LORA = 64
A_LORA = 64
G_LORA = 128
SB_COLS = 3 * WIDTH
RW_COLS = 3 * WIDTH + W_LORA + A_LORA + G_LORA
MIX_COLS = SB_COLS + RW_COLS
D_FF = 2816
RW_STREAMS = 5
RW_ROWS = 256
RMS_EPS = 1e-6
GN_EPS = HEAD_DIM * 1e-5
SB_TILE = 256
SB_SLOTS = 3
LOG2E = 1.4426950408889634
LANES = 128
SUBLANES = 8
MXU_N = 256
VMEM_LIMIT = 56 * 1024 * 1024


def _rms(x, g):
    ms = jnp.mean(x * x, axis=-1, keepdims=True)
    return x * lax.rsqrt(ms + RMS_EPS) * g


def _sigmoid(x):
    return 1.0 / (1.0 + jnp.exp(-x))


def _neg_abs(x):
    if x.dtype.itemsize != 4:
        return -jnp.abs(x)
    bits = lax.bitcast_convert_type(x, jnp.uint32) | jnp.uint32(0x80000000)
    return lax.bitcast_convert_type(bits, x.dtype)


def _softplus(x):
    return jnp.maximum(x, 0.0) + jnp.log(1.0 + jnp.exp(-jnp.abs(x)))


def _dot3(x, w_hi, w_lo):
    x_hi = x.astype(BF16)
    x_lo = (x - x_hi.astype(F32)).astype(BF16)
    return (jnp.dot(x_hi, w_hi, preferred_element_type=F32)
            + jnp.dot(x_lo, w_hi, preferred_element_type=F32)
            + jnp.dot(x_hi, w_lo, preferred_element_type=F32))


def _in_proj_kernel(tiles_per_seq, x_ref, g_ref, w_ref, mu_ref, w0_ref, a0_ref, wwa_hi_ref,
                    wwa_lo_ref, gup_hi_ref, gup_lo_ref, qkv_ref, rw_ref, gate_ref,
                    h_scr, p_scr, last_scr):
    tm = x_ref.shape[0]
    h_scr[...] = _rms(x_ref[...], g_ref[...]).astype(BF16)

    for c0 in list(range(SB_COLS, MIX_COLS, MXU_N)) + list(range(0, SB_COLS, MXU_N)):
        acc = jnp.dot(h_scr[...], w_ref[:, c0:c0 + MXU_N], preferred_element_type=F32)
        if c0 < WIDTH:
            acc = acc * (LOG2E * HEAD_DIM ** -0.5)
        if c0 < SB_COLS:
            qkv_ref[:, c0:c0 + MXU_N] = acc.astype(BF16)
        else:
            p_scr[:, c0 - SB_COLS:c0 - SB_COLS + MXU_N] = acc

    seq_start = pl.program_id(0) % tiles_per_seq == 0
    rowid = lax.broadcasted_iota(jnp.int32, (RW_ROWS, RW_COLS), 0)
    lane = lax.broadcasted_iota(jnp.int32, (RW_ROWS, W_LORA + A_LORA), 1)
    for r0 in range(0, tm, RW_ROWS):
        cur = p_scr[r0:r0 + RW_ROWS]
        if r0 == 0:
            before = jnp.where(seq_start, 0.0, last_scr[SUBLANES - 1:SUBLANES])
        else:
            before = p_scr[r0 - 1:r0]
        prev = jnp.where(rowid == 0, before, pltpu.roll(cur, 1, axis=0))
        p = cur + (prev - cur) * mu_ref[...]
        xwa = p[:, 3 * WIDTH:3 * WIDTH + W_LORA + A_LORA]
        xg = p[:, 3 * WIDTH + W_LORA + A_LORA:]
        twa = jnp.where(lane < W_LORA, jnp.tanh(xwa), xwa)
        wa = _dot3(twa, wwa_hi_ref[...], wwa_lo_ref[...])
        w_raw = w0_ref[...] + wa[:, :WIDTH]
        rs = slice(r0, r0 + RW_ROWS)
        rw_ref[0, rs] = p[:, 0:WIDTH]
        rw_ref[1, rs] = -jnp.exp(-_softplus(-w_raw) - 0.5)
        rw_ref[2, rs] = p[:, WIDTH:2 * WIDTH]
        rw_ref[3, rs] = p[:, 2 * WIDTH:3 * WIDTH]
        rw_ref[4, rs] = _sigmoid(a0_ref[...] + wa[:, WIDTH:])
        gate_ref[rs] = _dot3(_sigmoid(xg), gup_hi_ref[...], gup_lo_ref[...]).astype(BF16)
    last_scr[...] = p_scr[tm - SUBLANES:tm]


def _in_proj(xf, g, w, mu, w0, a0, wwa, g_up, seq_len, tm):
    T = xf.shape[0]
    spb = seq_len // tm
    hi = lambda a: a.astype(BF16)
    lo = lambda a: (a - a.astype(BF16).astype(F32)).astype(BF16)
    full = lambda a: pl.BlockSpec(a.shape, lambda i: (0, 0))
    consts = (g, w, mu, w0, a0, hi(wwa), lo(wwa), hi(g_up), lo(g_up))
    return pl.pallas_call(
        functools.partial(_in_proj_kernel, spb),
        grid=(T // tm,),
        in_specs=[pl.BlockSpec((tm, D_MODEL), lambda i: (i, 0))] + [full(a) for a in consts],
        out_specs=[
            pl.BlockSpec((tm, SB_COLS), lambda i: (i, 0)),
            pl.BlockSpec((RW_STREAMS, tm, WIDTH), lambda i: (0, i % spb, i // spb)),
            pl.BlockSpec((tm, WIDTH), lambda i: (i, 0)),
        ],
        out_shape=[
            jax.ShapeDtypeStruct((T, SB_COLS), BF16),
            jax.ShapeDtypeStruct((RW_STREAMS, seq_len, (T // seq_len) * WIDTH), F32),
            jax.ShapeDtypeStruct((T, WIDTH), BF16),
        ],
        scratch_shapes=[pltpu.VMEM((tm, D_MODEL), BF16), pltpu.VMEM((tm, RW_COLS), F32),
                        pltpu.VMEM((SUBLANES, RW_COLS), F32)],
        compiler_params=pltpu.CompilerParams(
            dimension_semantics=("arbitrary",), vmem_limit_bytes=VMEM_LIMIT),
        name="in_proj",
    )(xf, *consts)


def _sb_kernel(ti_ref, tj_ref, q_ref, k_ref, v_ref, o_ref, z_scr, c_scr, acc_scr, car_scr):
    S = q_ref.shape[1]
    nq = S // SB_TILE
    n_off = nq * (nq - 1) // 2
    row = lax.broadcasted_iota(jnp.int32, (SB_TILE, SB_TILE), 0)
    col = lax.broadcasted_iota(jnp.int32, (SB_TILE, SB_TILE), 1)
    incl_mat = jnp.where(row >= col, 1.0, 0.0).astype(BF16)
    strict = col < row
    head0 = lax.broadcasted_iota(jnp.int32, (SB_TILE, LANES), 1) < HEAD_DIM

    def rows(ref, t):
        return ref[0, pl.ds(pl.multiple_of(t * SB_TILE, SB_TILE), SB_TILE), :]

    def issue_logits(i, j, slot):
        q = rows(q_ref, i)
        kb = rows(k_ref, j)
        for h in range(2):
            qh = jnp.where(head0 if h == 0 else ~head0, q, jnp.zeros_like(q))
            z_scr[slot, h] = lax.dot_general(qh, kb, (((1,), (1,)), ((), ())),
                                             preferred_element_type=F32)

    def cumulate(slot, diag):
        for h in range(2):
            z = z_scr[slot, h]
            s = jnp.maximum(z, 0.0) + jnp.log(1.0 + jnp.exp2(_neg_abs(z))) * LOG2E
            if diag:
                s = jnp.where(strict, s, 0.0)
            c_scr[slot, h] = jnp.dot(s.astype(BF16), incl_mat, preferred_element_type=F32)

    def finish(i, j, slot, diag):
        vb = rows(v_ref, j)
        for h in range(2):
            c = c_scr[slot, h]
            e = z_scr[slot, h] - c
            if not diag:
                car = car_scr[i, h]
                e = e - jnp.concatenate([car, car], axis=1)
            w = jnp.exp2(e)
            if diag:
                w = jnp.where(strict, w, 0.0)
            pv = jnp.dot(w.astype(BF16), vb, preferred_element_type=F32)
            tot = jnp.broadcast_to(c[:, 0:1], (SB_TILE, LANES))
            if diag:
                acc_scr[i, h] = pv
                car_scr[i, h] = tot
            else:
                acc_scr[i, h] += pv
                car_scr[i, h] += tot

    def run(steps, tile_of, diag):
        def step(n, slot):
            issue_logits(*tile_of(jnp.minimum(n + 1, steps - 1)), (slot + 1) % SB_SLOTS)
            finish(*tile_of(n - 1), (slot - 1) % SB_SLOTS, diag)
            cumulate(slot, diag)

        issue_logits(*tile_of(0), 0)
        issue_logits(*tile_of(min(1, steps - 1)), 1)
        cumulate(0, diag)
        groups = (steps - 1) // SB_SLOTS

        def body(g, _):
            for u in range(SB_SLOTS):
                step(1 + SB_SLOTS * g + u, (1 + u) % SB_SLOTS)
            return 0

        lax.fori_loop(0, groups, body, 0)
        for n in range(1 + SB_SLOTS * groups, steps):
            step(n, n % SB_SLOTS)
        finish(*tile_of(steps - 1), (steps - 1) % SB_SLOTS, diag)

    run(nq, lambda n: (n, n), True)
    if n_off:
        run(n_off, lambda n: (ti_ref[n], tj_ref[n]), False)
    for i in range(nq):
        o_ref[0, i * SB_TILE:(i + 1) * SB_TILE, :] = jnp.where(
            head0, acc_scr[i, 0], acc_scr[i, 1]).astype(o_ref.dtype)


def _sb_attn(qkv):
    B, S, _ = qkv.shape
    npair = WIDTH // LANES
    nq = S // SB_TILE
    pairs = [(i, j) for i in range(nq) for j in range(i - 1, -1, -1)] or [(0, 0)]
    ti = jnp.array([p[0] for p in pairs], jnp.int32)
    tj = jnp.array([p[1] for p in pairs], jnp.int32)
    return pl.pallas_call(
        _sb_kernel,
        grid_spec=pltpu.PrefetchScalarGridSpec(
            num_scalar_prefetch=2,
            grid=(B, npair),
            in_specs=[
                pl.BlockSpec((1, S, LANES), lambda b, p, ti, tj: (b, 0, p)),
                pl.BlockSpec((1, S, LANES), lambda b, p, ti, tj: (b, 0, npair + p)),
                pl.BlockSpec((1, S, LANES), lambda b, p, ti, tj: (b, 0, 2 * npair + p)),
            ],
            out_specs=pl.BlockSpec((1, S, LANES), lambda b, p, ti, tj: (b, 0, p)),
            scratch_shapes=[
                pltpu.VMEM((SB_SLOTS, 2, SB_TILE, SB_TILE), F32),
                pltpu.VMEM((SB_SLOTS, 2, SB_TILE, SB_TILE), F32),
                pltpu.VMEM((nq, 2, SB_TILE, LANES), F32),
                pltpu.VMEM((nq, 2, SB_TILE, LANES), F32),
            ],
        ),
        out_shape=jax.ShapeDtypeStruct((B, S, WIDTH), BF16),
        compiler_params=pltpu.CompilerParams(
            dimension_semantics=("parallel", "parallel"), vmem_limit_bytes=VMEM_LIMIT),
        name="sb_attn",
    )(ti, tj, qkv, qkv, qkv)


def _wkv_kernel(r_ref, w_ref, k_ref, v_ref, a_ref, lw_ref, lb_ref, rk_ref, kk_ref, ka_ref, o_ref,
                st_ref, y_scr, x_scr, o_scr, ld_scr):
    @pl.when(pl.program_id(1) == 0)
    def _():
        st_ref[...] = jnp.zeros_like(st_ref)

    ld_scr[...] = jnp.zeros_like(ld_scr)
    tc = r_ref.shape[0]
    npairs = tc // 2
    low = lax.broadcasted_iota(jnp.int32, (HEAD_DIM, LANES), 1) < HEAD_DIM
    swap = lambda x: pltpu.roll(x, HEAD_DIM, axis=1)
    sub = lax.broadcasted_iota(jnp.int32, (SUBLANES, LANES), 0)

    def fold(a, b, sh):
        first = (sub & sh) == 0
        if 2 * sh == SUBLANES:
            return jnp.where(first, a, b) + pltpu.roll(jnp.where(first, b, a), sh, axis=0)
        return jnp.where(first, a + pltpu.roll(a, SUBLANES - sh, axis=0),
                         b + pltpu.roll(b, sh, axis=0))

    def to_streams(p):
        t0 = pl.multiple_of(2 * p, 2)
        for n, ref in enumerate((r_ref, w_ref, k_ref, v_ref, a_ref)):
            xt = ref[pl.ds(t0, 2)].reshape(LANES, LANES).T
            x_scr[n, t0] = jnp.where(low, xt[:HEAD_DIM], swap(xt[HEAD_DIM:]))
            x_scr[n, t0 + 1] = jnp.where(low, swap(xt[:HEAD_DIM]), xt[HEAD_DIM:])

    def to_tokens(p):
        t0 = pl.multiple_of(2 * p, 2)
        ya = o_scr[t0]
        yb = o_scr[t0 + 1]
        m = jnp.concatenate([jnp.where(low, ya, swap(yb)), jnp.where(low, swap(ya), yb)], axis=0)
        o_ref[pl.ds(t0, 2)] = m.T.reshape(2, HEAD_DIM, LANES)

    def step(t):
        a = x_scr[4, t]
        k = x_scr[2, t]
        kk = k * kk_ref[...]
        k = k * (1.0 + (a - 1.0) * ka_ref[...])
        norm = jnp.sqrt(jnp.sum(kk * kk, axis=0, keepdims=True))
        kkn = kk * (1.0 / jnp.maximum(norm, 1e-12))
        r = x_scr[0, t]
        vt = x_scr[3, t]
        g_before = jnp.exp(ld_scr[...])
        ld = ld_scr[...] + x_scr[1, t]
        ld_scr[...] = ld
        g_now = jnp.exp(ld)
        g_inv = jnp.exp(-ld)
        nkk = -kkn * g_before
        bb = kkn * a * g_inv
        ks = k * g_inv
        rs = r * g_now
        for vb in range(HEAD_DIM // SUBLANES):
            part = []
            for j in range(SUBLANES):
                vi = vb * SUBLANES + j
                s = st_ref[vi]
                sa = jnp.sum(s * nkk, axis=0, keepdims=True)
                s = s + sa * bb + vt[vi:vi + 1] * ks
                st_ref[vi] = s
                part.append(jnp.sum((s * rs).reshape(SUBLANES, SUBLANES, LANES), axis=0))
            y_scr[vb * SUBLANES:(vb + 1) * SUBLANES, :] = fold(
                fold(fold(part[0], part[4], 4), fold(part[2], part[6], 4), 2),
                fold(fold(part[1], part[5], 4), fold(part[3], part[7], 4), 2), 1)
        y = y_scr[...]
        d = y - jnp.mean(y, axis=0, keepdims=True)
        var = jnp.mean(d * d, axis=0, keepdims=True)
        bonus = jnp.sum(r * k * rk_ref[...], axis=0, keepdims=True) * vt
        o_scr[t] = d * lax.rsqrt(var + GN_EPS) * lw_ref[...] + lb_ref[...] + bonus

    to_streams(0)
    step(0)
    step(1)
    to_streams(min(1, npairs - 1))

    def body(p, _):
        to_tokens(p - 1)
        step(2 * p)
        step(2 * p + 1)
        to_streams(jnp.minimum(p + 1, npairs - 1))
        return 0

    lax.fori_loop(1, npairs, body, 0)
    to_tokens(npairs - 1)
    g_end = jnp.exp(ld_scr[...])
    for vi in range(HEAD_DIM):
        st_ref[vi] = st_ref[vi] * g_end


def _wkv(rw, lw, lb, rk, kkc, kac, tc):
    _, S, R, _ = rw.shape
    N = HEAD_DIM
    inp = lambda n: pl.BlockSpec((None, tc, N, LANES), lambda g, c: (n, c, g, 0))
    const = pl.BlockSpec((N, LANES), lambda g, c: (0, 0))
    return pl.pallas_call(
        _wkv_kernel,
        grid=(R // N, S // tc),
        in_specs=[inp(n) for n in range(RW_STREAMS)] + [const] * 5,
        out_specs=pl.BlockSpec((tc, N, LANES), lambda g, c: (c, g, 0)),
        out_shape=jax.ShapeDtypeStruct((S, R, LANES), F32),
        scratch_shapes=[pltpu.VMEM((N, N, LANES), F32), pltpu.VMEM((N, LANES), F32),
                        pltpu.VMEM((RW_STREAMS, tc, N, LANES), F32),
                        pltpu.VMEM((tc, N, LANES), F32), pltpu.VMEM((N, LANES), F32)],
        compiler_params=pltpu.CompilerParams(
            dimension_semantics=("parallel", "arbitrary"), vmem_limit_bytes=VMEM_LIMIT),
        name="wkv",
    )(*([rw] * RW_STREAMS), lw, lb, rk, kkc, kac)


def _merge_kernel(x_ref, osb_ref, y_ref, g_ref, gpre_ref, wg_ref, bg_ref, wsb_ref, wrw_ref,
                  wo_ref, gpost_ref, x1_ref, h_scr, orw_scr, m_scr, f_scr):
    x = x_ref[...]
    h_scr[...] = _rms(x, gpre_ref[...]).astype(BF16)
    orw_scr[...] = (y_ref[...] * g_ref[...].astype(F32)).astype(BF16)
    for c0 in range(0, D_MODEL, MXU_N):
        cs = slice(c0, c0 + MXU_N)
        cr = slice(D_MODEL + c0, D_MODEL + c0 + MXU_N)
        g_sb = _sigmoid(jnp.dot(h_scr[...], wg_ref[:, cs], preferred_element_type=F32)
                        + bg_ref[:, cs])
        g_rw = _sigmoid(jnp.dot(h_scr[...], wg_ref[:, cr], preferred_element_type=F32)
                        + bg_ref[:, cr])
        a_sb = jnp.dot(osb_ref[...], wsb_ref[:, cs], preferred_element_type=F32)
        a_rw = jnp.dot(orw_scr[...], wrw_ref[:, cs], preferred_element_type=F32)
        m_scr[:, cs] = (g_sb * a_sb + g_rw * a_rw).astype(BF16)
    for c0 in range(0, D_MODEL, MXU_N):
        cs = slice(c0, c0 + MXU_N)
        f_scr[:, cs] = jnp.dot(m_scr[...], wo_ref[:, cs], preferred_element_type=F32)
    x1_ref[...] = x + _rms(f_scr[...], gpost_ref[...])


def _merge(xf, o_sb, y, g, gpre, wg, bg, wsb, wrw, wo, gpost, tm):
    T = xf.shape[0]
    spb = y.shape[0] // tm
    tile = lambda c: pl.BlockSpec((tm, c), lambda i: (i, 0))
    y_tile = pl.BlockSpec((tm, WIDTH), lambda i: (i % spb, i // spb))
    full = lambda a: pl.BlockSpec(a.shape, lambda i: (0, 0))
    return pl.pallas_call(
        _merge_kernel,
        grid=(T // tm,),
        in_specs=[tile(D_MODEL), tile(WIDTH), y_tile, tile(WIDTH), full(gpre), full(wg),
                  full(bg), full(wsb), full(wrw), full(wo), full(gpost)],
        out_specs=tile(D_MODEL),
        out_shape=jax.ShapeDtypeStruct((T, D_MODEL), F32),
        scratch_shapes=[pltpu.VMEM((tm, D_MODEL), BF16), pltpu.VMEM((tm, WIDTH), BF16),
                        pltpu.VMEM((tm, D_MODEL), BF16), pltpu.VMEM((tm, D_MODEL), F32)],
        compiler_params=pltpu.CompilerParams(
            dimension_semantics=("parallel",), vmem_limit_bytes=VMEM_LIMIT),
        name="merge",
    )(xf, o_sb, y, g, gpre, wg, bg, wsb, wrw, wo, gpost)


def _ffn_kernel(x_ref, gpre_ref, wg_ref, wu_ref, wd_ref, gpost_ref, o_ref, h_scr, acc_scr):
    x = x_ref[...]
    h_scr[...] = _rms(x, gpre_ref[...]).astype(BF16)
    for c0 in range(0, D_FF, MXU_N):
        cs = slice(c0, c0 + MXU_N)
        gt = jnp.dot(h_scr[...], wg_ref[:, cs], preferred_element_type=F32)
        up = jnp.dot(h_scr[...], wu_ref[:, cs], preferred_element_type=F32)
        act = (gt * _sigmoid(gt) * up).astype(BF16)
        part = jnp.dot(act, wd_ref[cs, :], preferred_element_type=F32)
        if c0 == 0:
            acc_scr[...] = part
        else:
            acc_scr[...] += part
    o_ref[...] = x + _rms(acc_scr[...], gpost_ref[...])


def _ffn(x1, gpre, wg, wu, wd, gpost, tm):
    T = x1.shape[0]
    tile = pl.BlockSpec((tm, D_MODEL), lambda i: (i, 0))
    full = lambda a: pl.BlockSpec(a.shape, lambda i: (0, 0))
    return pl.pallas_call(
        _ffn_kernel,
        grid=(T // tm,),
        in_specs=[tile, full(gpre), full(wg), full(wu), full(wd), full(gpost)],
        out_specs=tile,
        out_shape=jax.ShapeDtypeStruct((T, D_MODEL), F32),
        scratch_shapes=[pltpu.VMEM((tm, D_MODEL), BF16), pltpu.VMEM((tm, D_MODEL), F32)],
        compiler_params=pltpu.CompilerParams(
            dimension_semantics=("parallel",), vmem_limit_bytes=VMEM_LIMIT),
        name="ffn",
    )(x1, gpre, wg, wu, wd, gpost)


def _layer(x, norm_mix_pre, w_in, b_gate, mu_rw, w0, w_up, a0, a_up, g_up, k_k, k_a, r_k,
           lnx_w, lnx_b, w_sb_out, w_rw_out, w_o, norm_mix_post, norm_ffn_pre,
           w_ffn_gate, w_ffn_up, w_ffn_down, norm_ffn_post):
    B, S, D = x.shape
    T = B * S
    NS = B * HEADS
    assert D == D_MODEL and S % SB_TILE == 0 and NS % LANES == 0
    tm = min(512, S)
    row = lambda a: a.reshape(1, -1)

    xf = x.reshape(T, D)
    w_in_b = w_in.astype(BF16)
    wwa = jnp.zeros((W_LORA + A_LORA, 2 * WIDTH), F32)
    wwa = wwa.at[:W_LORA, :WIDTH].set(w_up).at[W_LORA:, WIDTH:].set(a_up)
    qkv, rw, g = _in_proj(xf, row(norm_mix_pre), w_in_b[:, :MIX_COLS], row(mu_rw), row(w0),
                          row(a0), wwa, g_up, S, tm)

    o_sb = _sb_attn(qkv.reshape(B, S, SB_COLS))

    lane_head = 2 * (jnp.arange(LANES) % (HEADS // 2)) + jnp.arange(LANES) // HEAD_DIM
    per_stream = lambda a: a.reshape(HEADS, HEAD_DIM)[lane_head].T
    y = _wkv(rw.reshape(RW_STREAMS, S, B * WIDTH // LANES, LANES), per_stream(lnx_w),
             per_stream(lnx_b), per_stream(r_k), per_stream(k_k), per_stream(k_a), min(64, S))
    y = y.reshape(S, B * WIDTH)

    x1 = _merge(xf, o_sb.reshape(T, WIDTH), y, g, row(norm_mix_pre),
                w_in_b[:, MIX_COLS:], row(b_gate), w_sb_out.astype(BF16),
                w_rw_out.astype(BF16), w_o.astype(BF16), row(norm_mix_post), tm)
    out = _ffn(x1, row(norm_ffn_pre), w_ffn_gate.astype(BF16), w_ffn_up.astype(BF16),
               w_ffn_down.astype(BF16), row(norm_ffn_post), tm)
    return out.reshape(B, S, D)


def kernel(x, norm_mix_pre, w_in, b_gate, mu_rw, w0, w_up, a0, a_up, g_up, k_k, k_a, r_k,
           lnx_w, lnx_b, w_sb_out, w_rw_out, w_o, norm_mix_post, norm_ffn_pre,
           w_ffn_gate, w_ffn_up, w_ffn_down, norm_ffn_post):
    params = (norm_mix_pre, w_in, b_gate, mu_rw, w0, w_up, a0, a_up, g_up, k_k, k_a, r_k,
              lnx_w, lnx_b, w_sb_out, w_rw_out, w_o, norm_mix_post, norm_ffn_pre,
              w_ffn_gate, w_ffn_up, w_ffn_down, norm_ffn_post)
    for l in range(norm_mix_pre.shape[0]):
        x = _layer(x, *(p[l] for p in params))
    return x
```

```python
import functools

import jax
import jax.numpy as jnp
from jax import lax
from jax.experimental import pallas as pl
from jax.experimental.pallas import tpu as pltpu

F32 = jnp.float32
BF16 = jnp.bfloat16

D_MODEL = 1024
HEADS = 8
HEAD_DIM = 64
WIDTH = HEADS * HEAD_DIM
W_LORA = 64
A_LORA = 64
G_LORA = 128
SB_COLS = 3 * WIDTH
RW_COLS = 3 * WIDTH + W_LORA + A_LORA + G_LORA
MIX_COLS = SB_COLS + RW_COLS
D_FF = 2816
RW_STREAMS = 5
NVB = HEAD_DIM // 8
VB_GROUP = 2
N_CHAINS = 4
RW_ROWS = 256
RMS_EPS = 1e-6
GN_EPS = HEAD_DIM * 1e-5
SB_TILE = 256
SB_SLOTS = 3
LOG2E = 1.4426950408889634
LANES = 128
SUBLANES = 8
MXU_N = 256
VMEM_LIMIT = 56 * 1024 * 1024


def _rms(x, g):
    ms = jnp.mean(x * x, axis=-1, keepdims=True)
    return x * lax.rsqrt(ms + RMS_EPS) * g


def _sigmoid(x):
    return 1.0 / (1.0 + jnp.exp(-x))


def _neg_abs(x):
    if x.dtype.itemsize != 4:
        return -jnp.abs(x)
    bits = lax.bitcast_convert_type(x, jnp.uint32) | jnp.uint32(0x80000000)
    return lax.bitcast_convert_type(bits, x.dtype)


def _tree_sum(xs):
    while len(xs) > 1:
        xs = [a + b for a, b in zip(xs[::2], xs[1::2])] + ([xs[-1]] if len(xs) % 2 else [])
    return xs[0]


def _softplus(x):
    return jnp.maximum(x, 0.0) + jnp.log(1.0 + jnp.exp(-jnp.abs(x)))


def _dot3(x, w_hi, w_lo):
    x_hi = x.astype(BF16)
    x_lo = (x - x_hi.astype(F32)).astype(BF16)
    return (jnp.dot(x_hi, w_hi, preferred_element_type=F32)
            + jnp.dot(x_lo, w_hi, preferred_element_type=F32)
            + jnp.dot(x_hi, w_lo, preferred_element_type=F32))


def _in_proj_kernel(tiles_per_seq, x_ref, g_ref, w_ref, mu_ref, w0_ref, a0_ref, wwa_hi_ref,
                    wwa_lo_ref, gup_hi_ref, gup_lo_ref, qkv_ref, rw_ref, gate_ref,
                    h_scr, p_scr, last_scr):
    tm = x_ref.shape[0]
    h_scr[...] = _rms(x_ref[...], g_ref[...]).astype(BF16)

    for c0 in list(range(SB_COLS, MIX_COLS, MXU_N)) + list(range(0, SB_COLS, MXU_N)):
        acc = jnp.dot(h_scr[...], w_ref[:, c0:c0 + MXU_N], preferred_element_type=F32)
        if c0 < WIDTH:
            acc = acc * (LOG2E * HEAD_DIM ** -0.5)
        if c0 < SB_COLS:
            qkv_ref[:, c0:c0 + MXU_N] = acc.astype(BF16)
        else:
            p_scr[:, c0 - SB_COLS:c0 - SB_COLS + MXU_N] = acc

    seq_start = pl.program_id(0) % tiles_per_seq == 0
    rowid = lax.broadcasted_iota(jnp.int32, (RW_ROWS, RW_COLS), 0)
    lane = lax.broadcasted_iota(jnp.int32, (RW_ROWS, W_LORA + A_LORA), 1)
    for r0 in range(0, tm, RW_ROWS):
        cur = p_scr[r0:r0 + RW_ROWS]
        if r0 == 0:
            before = jnp.where(seq_start, 0.0, last_scr[SUBLANES - 1:SUBLANES])
        else:
            before = p_scr[r0 - 1:r0]
        prev = jnp.where(rowid == 0, before, pltpu.roll(cur, 1, axis=0))
        p = cur + (prev - cur) * mu_ref[...]
        xwa = p[:, 3 * WIDTH:3 * WIDTH + W_LORA + A_LORA]
        xg = p[:, 3 * WIDTH + W_LORA + A_LORA:]
        twa = jnp.where(lane < W_LORA, jnp.tanh(xwa), xwa)
        wa = _dot3(twa, wwa_hi_ref[...], wwa_lo_ref[...])
        w_raw = w0_ref[...] + wa[:, :WIDTH]
        rs = slice(r0, r0 + RW_ROWS)
        rw_ref[0, rs] = p[:, 0:WIDTH]
        rw_ref[1, rs] = -jnp.exp(-_softplus(-w_raw) - 0.5)
        rw_ref[2, rs] = p[:, WIDTH:2 * WIDTH]
        rw_ref[3, rs] = p[:, 2 * WIDTH:3 * WIDTH]
        rw_ref[4, rs] = _sigmoid(a0_ref[...] + wa[:, WIDTH:])
        gate_ref[rs] = _dot3(_sigmoid(xg), gup_hi_ref[...], gup_lo_ref[...]).astype(BF16)
    last_scr[...] = p_scr[tm - SUBLANES:tm]


def _in_proj(xf, g, w, mu, w0, a0, wwa, g_up, seq_len, tm):
    T = xf.shape[0]
    spb = seq_len // tm
    hi = lambda a: a.astype(BF16)
    lo = lambda a: (a - a.astype(BF16).astype(F32)).astype(BF16)
    full = lambda a: pl.BlockSpec(a.shape, lambda i: (0, 0))
    consts = (g, w, mu, w0, a0, hi(wwa), lo(wwa), hi(g_up), lo(g_up))
    return pl.pallas_call(
        functools.partial(_in_proj_kernel, spb),
        grid=(T // tm,),
        in_specs=[pl.BlockSpec((tm, D_MODEL), lambda i: (i, 0))] + [full(a) for a in consts],
        out_specs=[
            pl.BlockSpec((tm, SB_COLS), lambda i: (i, 0)),
            pl.BlockSpec((RW_STREAMS, tm, WIDTH), lambda i: (0, i % spb, i // spb)),
            pl.BlockSpec((tm, WIDTH), lambda i: (i, 0)),
        ],
        out_shape=[
            jax.ShapeDtypeStruct((T, SB_COLS), BF16),
            jax.ShapeDtypeStruct((RW_STREAMS, seq_len, (T // seq_len) * WIDTH), F32),
            jax.ShapeDtypeStruct((T, WIDTH), BF16),
        ],
        scratch_shapes=[pltpu.VMEM((tm, D_MODEL), BF16), pltpu.VMEM((tm, RW_COLS), F32),
                        pltpu.VMEM((SUBLANES, RW_COLS), F32)],
        compiler_params=pltpu.CompilerParams(
            dimension_semantics=("arbitrary",), vmem_limit_bytes=VMEM_LIMIT),
        name="in_proj",
    )(xf, *consts)


def _sb_kernel(ti_ref, tj_ref, q_ref, k_ref, v_ref, o_ref, z_scr, c_scr, acc_scr, car_scr):
    S = q_ref.shape[1]
    nq = S // SB_TILE
    n_off = nq * (nq - 1) // 2
    row = lax.broadcasted_iota(jnp.int32, (SB_TILE, SB_TILE), 0)
    col = lax.broadcasted_iota(jnp.int32, (SB_TILE, SB_TILE), 1)
    incl_mat = jnp.where(row >= col, 1.0, 0.0).astype(BF16)
    strict = col < row
    head0 = lax.broadcasted_iota(jnp.int32, (SB_TILE, LANES), 1) < HEAD_DIM

    def rows(ref, t):
        return ref[0, pl.ds(pl.multiple_of(t * SB_TILE, SB_TILE), SB_TILE), :]

    def issue_logits(i, j, slot):
        q = rows(q_ref, i)
        kb = rows(k_ref, j)
        for h in range(2):
            qh = jnp.where(head0 if h == 0 else ~head0, q, jnp.zeros_like(q))
            z_scr[slot, h] = lax.dot_general(qh, kb, (((1,), (1,)), ((), ())),
                                             preferred_element_type=F32)

    def cumulate(slot, diag):
        for h in range(2):
            z = z_scr[slot, h]
            s = jnp.maximum(z, 0.0) + jnp.log(1.0 + jnp.exp2(_neg_abs(z))) * LOG2E
            if diag:
                s = jnp.where(strict, s, 0.0)
            c_scr[slot, h] = jnp.dot(s.astype(BF16), incl_mat, preferred_element_type=F32)

    def finish(i, j, slot, diag):
        vb = rows(v_ref, j)
        for h in range(2):
            c = c_scr[slot, h]
            e = z_scr[slot, h] - c
            if not diag:
                car = car_scr[i, h]
                e = e - jnp.concatenate([car, car], axis=1)
            w = jnp.exp2(e)
            if diag:
                w = jnp.where(strict, w, 0.0)
            pv = jnp.dot(w.astype(BF16), vb, preferred_element_type=F32)
            tot = jnp.broadcast_to(c[:, 0:1], (SB_TILE, LANES))
            if diag:
                acc_scr[i, h] = pv
                car_scr[i, h] = tot
            else:
                acc_scr[i, h] += pv
                car_scr[i, h] += tot

    def run(steps, tile_of, diag):
        def step(n, slot):
            issue_logits(*tile_of(jnp.minimum(n + 1, steps - 1)), (slot + 1) % SB_SLOTS)
            finish(*tile_of(n - 1), (slot - 1) % SB_SLOTS, diag)
            cumulate(slot, diag)

        issue_logits(*tile_of(0), 0)
        issue_logits(*tile_of(min(1, steps - 1)), 1)
        cumulate(0, diag)
        groups = (steps - 1) // SB_SLOTS

        def body(g, _):
            for u in range(SB_SLOTS):
                step(1 + SB_SLOTS * g + u, (1 + u) % SB_SLOTS)
            return 0

        lax.fori_loop(0, groups, body, 0)
        for n in range(1 + SB_SLOTS * groups, steps):
            step(n, n % SB_SLOTS)
        finish(*tile_of(steps - 1), (steps - 1) % SB_SLOTS, diag)

    run(nq, lambda n: (n, n), True)
    if n_off:
        run(n_off, lambda n: (ti_ref[n], tj_ref[n]), False)
    for i in range(nq):
        o_ref[0, i * SB_TILE:(i + 1) * SB_TILE, :] = jnp.where(
            head0, acc_scr[i, 0], acc_scr[i, 1]).astype(o_ref.dtype)


def _sb_attn(qkv):
    B, S, _ = qkv.shape
    npair = WIDTH // LANES
    nq = S // SB_TILE
    pairs = [(i, j) for i in range(nq) for j in range(i - 1, -1, -1)] or [(0, 0)]
    ti = jnp.array([p[0] for p in pairs], jnp.int32)
    tj = jnp.array([p[1] for p in pairs], jnp.int32)
    return pl.pallas_call(
        _sb_kernel,
        grid_spec=pltpu.PrefetchScalarGridSpec(
            num_scalar_prefetch=2,
            grid=(B, npair),
            in_specs=[
                pl.BlockSpec((1, S, LANES), lambda b, p, ti, tj: (b, 0, p)),
                pl.BlockSpec((1, S, LANES), lambda b, p, ti, tj: (b, 0, npair + p)),
                pl.BlockSpec((1, S, LANES), lambda b, p, ti, tj: (b, 0, 2 * npair + p)),
            ],
            out_specs=pl.BlockSpec((1, S, LANES), lambda b, p, ti, tj: (b, 0, p)),
            scratch_shapes=[
                pltpu.VMEM((SB_SLOTS, 2, SB_TILE, SB_TILE), F32),
                pltpu.VMEM((SB_SLOTS, 2, SB_TILE, SB_TILE), F32),
                pltpu.VMEM((nq, 2, SB_TILE, LANES), F32),
                pltpu.VMEM((nq, 2, SB_TILE, LANES), F32),
            ],
        ),
        out_shape=jax.ShapeDtypeStruct((B, S, WIDTH), BF16),
        compiler_params=pltpu.CompilerParams(
            dimension_semantics=("parallel", "parallel"), vmem_limit_bytes=VMEM_LIMIT),
        name="sb_attn",
    )(ti, tj, qkv, qkv, qkv)


def _wkv_kernel(r_ref, w_ref, k_ref, v_ref, a_ref, lw_ref, lb_ref, rk_ref, kk_ref, ka_ref, o_ref,
                st_ref, y_scr, x_scr, o_scr, ld_scr, vec_scr):
    @pl.when(pl.program_id(1) == 0)
    def _():
        st_ref[...] = jnp.zeros_like(st_ref)

    ld_scr[...] = jnp.zeros_like(ld_scr)
    tc = r_ref.shape[0]
    npairs = tc // 2
    low = lax.broadcasted_iota(jnp.int32, (HEAD_DIM, LANES), 1) < HEAD_DIM
    swap = lambda x: pltpu.roll(x, HEAD_DIM, axis=1)
    def to_streams(p):
        t0 = pl.multiple_of(2 * p, 2)
        for n, ref in enumerate((r_ref, w_ref, k_ref, v_ref, a_ref)):
            xt = ref[pl.ds(t0, 2)].reshape(LANES, LANES).T
            x_scr[n, t0] = jnp.where(low, xt[:HEAD_DIM], swap(xt[HEAD_DIM:]))
            x_scr[n, t0 + 1] = jnp.where(low, swap(xt[:HEAD_DIM]), xt[HEAD_DIM:])

    def to_tokens(p):
        t0 = pl.multiple_of(2 * p, 2)
        ya = o_scr[t0]
        yb = o_scr[t0 + 1]
        m = jnp.concatenate([jnp.where(low, ya, swap(yb)), jnp.where(low, swap(ya), yb)], axis=0)
        o_ref[pl.ds(t0, 2)] = m.T.reshape(2, HEAD_DIM, LANES)

    def step(t):
        a = x_scr[4, t]
        k = x_scr[2, t]
        kk = k * kk_ref[...]
        k = k * (1.0 + (a - 1.0) * ka_ref[...])
        norm = jnp.sqrt(jnp.sum(kk * kk, axis=0, keepdims=True))
        kkn = kk * (1.0 / jnp.maximum(norm, 1e-12))
        r = x_scr[0, t]
        vt = x_scr[3, t]
        g_before = jnp.exp(ld_scr[...])
        ld = ld_scr[...] + x_scr[1, t]
        ld_scr[...] = ld
        g_now = jnp.exp(ld)
        g_inv = jnp.exp(-ld)
        vec_scr[0] = -kkn * g_before
        vec_scr[1] = kkn * a * g_inv
        vec_scr[2] = k * g_inv
        vec_scr[3] = r * g_now
        chan = lambda i, kc: vec_scr[i, kc:kc + 1, :]
        for vb0 in range(0, NVB, VB_GROUP):
            blocks = range(vb0, vb0 + VB_GROUP)
            acc = {(b, c): None for b in blocks for c in range(N_CHAINS)}
            for kc in range(HEAD_DIM):
                nrow = chan(0, kc)
                for b in blocks:
                    key = (b, kc % N_CHAINS)
                    prod = st_ref[b, kc] * nrow
                    acc[key] = prod if acc[key] is None else acc[key] + prod
            sa = {b: _tree_sum([acc[(b, c)] for c in range(N_CHAINS)]) for b in blocks}
            vrow = {b: vt[b * SUBLANES:(b + 1) * SUBLANES] for b in blocks}
            acc = {(b, c): None for b in blocks for c in range(N_CHAINS)}
            for kc in range(HEAD_DIM):
                brow, krow, rrow = chan(1, kc), chan(2, kc), chan(3, kc)
                for b in blocks:
                    key = (b, kc % N_CHAINS)
                    s = st_ref[b, kc] + sa[b] * brow + vrow[b] * krow
                    st_ref[b, kc] = s
                    prod = s * rrow
                    acc[key] = prod if acc[key] is None else acc[key] + prod
            for b in blocks:
                y_scr[b * SUBLANES:(b + 1) * SUBLANES, :] = _tree_sum(
                    [acc[(b, c)] for c in range(N_CHAINS)])
        y = y_scr[...]
        d = y - jnp.mean(y, axis=0, keepdims=True)
        var = jnp.mean(d * d, axis=0, keepdims=True)
        bonus = jnp.sum(r * k * rk_ref[...], axis=0, keepdims=True) * vt
        o_scr[t] = d * lax.rsqrt(var + GN_EPS) * lw_ref[...] + lb_ref[...] + bonus

    to_streams(0)
    step(0)
    step(1)
    to_streams(min(1, npairs - 1))

    def body(p, _):
        to_tokens(p - 1)
        step(2 * p)
        step(2 * p + 1)
        to_streams(jnp.minimum(p + 1, npairs - 1))
        return 0

    lax.fori_loop(1, npairs, body, 0)
    to_tokens(npairs - 1)
    vec_scr[0] = jnp.exp(ld_scr[...])
    for vb in range(NVB):
        for kc in range(HEAD_DIM):
            st_ref[vb, kc] = st_ref[vb, kc] * vec_scr[0, kc:kc + 1, :]


def _wkv(rw, lw, lb, rk, kkc, kac, tc):
    _, S, R, _ = rw.shape
    N = HEAD_DIM
    inp = lambda n: pl.BlockSpec((None, tc, N, LANES), lambda g, c: (n, c, g, 0))
    const = pl.BlockSpec((N, LANES), lambda g, c: (0, 0))
    return pl.pallas_call(
        _wkv_kernel,
        grid=(R // N, S // tc),
        in_specs=[inp(n) for n in range(RW_STREAMS)] + [const] * 5,
        out_specs=pl.BlockSpec((tc, N, LANES), lambda g, c: (c, g, 0)),
        out_shape=jax.ShapeDtypeStruct((S, R, LANES), F32),
        scratch_shapes=[pltpu.VMEM((NVB, N, SUBLANES, LANES), F32), pltpu.VMEM((N, LANES), F32),
                        pltpu.VMEM((RW_STREAMS, tc, N, LANES), F32),
                        pltpu.VMEM((tc, N, LANES), F32), pltpu.VMEM((N, LANES), F32),
                        pltpu.VMEM((4, N, LANES), F32)],
        compiler_params=pltpu.CompilerParams(
            dimension_semantics=("parallel", "arbitrary"), vmem_limit_bytes=VMEM_LIMIT),
        name="wkv",
    )(*([rw] * RW_STREAMS), lw, lb, rk, kkc, kac)


def _merge_kernel(x_ref, osb_ref, y_ref, g_ref, gpre_ref, wg_ref, bg_ref, wsb_ref, wrw_ref,
                  wo_ref, gpost_ref, x1_ref, h_scr, orw_scr, m_scr, f_scr):
    x = x_ref[...]
    h_scr[...] = _rms(x, gpre_ref[...]).astype(BF16)
    orw_scr[...] = (y_ref[...] * g_ref[...].astype(F32)).astype(BF16)
    for c0 in range(0, D_MODEL, MXU_N):
        cs = slice(c0, c0 + MXU_N)
        cr = slice(D_MODEL + c0, D_MODEL + c0 + MXU_N)
        g_sb = _sigmoid(jnp.dot(h_scr[...], wg_ref[:, cs], preferred_element_type=F32)
                        + bg_ref[:, cs])
        g_rw = _sigmoid(jnp.dot(h_scr[...], wg_ref[:, cr], preferred_element_type=F32)
                        + bg_ref[:, cr])
        a_sb = jnp.dot(osb_ref[...], wsb_ref[:, cs], preferred_element_type=F32)
        a_rw = jnp.dot(orw_scr[...], wrw_ref[:, cs], preferred_element_type=F32)
        m_scr[:, cs] = (g_sb * a_sb + g_rw * a_rw).astype(BF16)
    for c0 in range(0, D_MODEL, MXU_N):
        cs = slice(c0, c0 + MXU_N)
        f_scr[:, cs] = jnp.dot(m_scr[...], wo_ref[:, cs], preferred_element_type=F32)
    x1_ref[...] = x + _rms(f_scr[...], gpost_ref[...])


def _merge(xf, o_sb, y, g, gpre, wg, bg, wsb, wrw, wo, gpost, tm):
    T = xf.shape[0]
    spb = y.shape[0] // tm
    tile = lambda c: pl.BlockSpec((tm, c), lambda i: (i, 0))
    y_tile = pl.BlockSpec((tm, WIDTH), lambda i: (i % spb, i // spb))
    full = lambda a: pl.BlockSpec(a.shape, lambda i: (0, 0))
    return pl.pallas_call(
        _merge_kernel,
        grid=(T // tm,),
        in_specs=[tile(D_MODEL), tile(WIDTH), y_tile, tile(WIDTH), full(gpre), full(wg),
                  full(bg), full(wsb), full(wrw), full(wo), full(gpost)],
        out_specs=tile(D_MODEL),
        out_shape=jax.ShapeDtypeStruct((T, D_MODEL), F32),
        scratch_shapes=[pltpu.VMEM((tm, D_MODEL), BF16), pltpu.VMEM((tm, WIDTH), BF16),
                        pltpu.VMEM((tm, D_MODEL), BF16), pltpu.VMEM((tm, D_MODEL), F32)],
        compiler_params=pltpu.CompilerParams(
            dimension_semantics=("parallel",), vmem_limit_bytes=VMEM_LIMIT),
        name="merge",
    )(xf, o_sb, y, g, gpre, wg, bg, wsb, wrw, wo, gpost)


def _ffn_kernel(x_ref, gpre_ref, wg_ref, wu_ref, wd_ref, gpost_ref, o_ref, h_scr, acc_scr):
    x = x_ref[...]
    h_scr[...] = _rms(x, gpre_ref[...]).astype(BF16)
    for c0 in range(0, D_FF, MXU_N):
        cs = slice(c0, c0 + MXU_N)
        gt = jnp.dot(h_scr[...], wg_ref[:, cs], preferred_element_type=F32)
        up = jnp.dot(h_scr[...], wu_ref[:, cs], preferred_element_type=F32)
        act = (gt * _sigmoid(gt) * up).astype(BF16)
        part = jnp.dot(act, wd_ref[cs, :], preferred_element_type=F32)
        if c0 == 0:
            acc_scr[...] = part
        else:
            acc_scr[...] += part
    o_ref[...] = x + _rms(acc_scr[...], gpost_ref[...])


def _ffn(x1, gpre, wg, wu, wd, gpost, tm):
    T = x1.shape[0]
    tile = pl.BlockSpec((tm, D_MODEL), lambda i: (i, 0))
    full = lambda a: pl.BlockSpec(a.shape, lambda i: (0, 0))
    return pl.pallas_call(
        _ffn_kernel,
        grid=(T // tm,),
        in_specs=[tile, full(gpre), full(wg), full(wu), full(wd), full(gpost)],
        out_specs=tile,
        out_shape=jax.ShapeDtypeStruct((T, D_MODEL), F32),
        scratch_shapes=[pltpu.VMEM((tm, D_MODEL), BF16), pltpu.VMEM((tm, D_MODEL), F32)],
        compiler_params=pltpu.CompilerParams(
            dimension_semantics=("parallel",), vmem_limit_bytes=VMEM_LIMIT),
        name="ffn",
    )(x1, gpre, wg, wu, wd, gpost)


def _layer(x, norm_mix_pre, w_in, b_gate, mu_rw, w0, w_up, a0, a_up, g_up, k_k, k_a, r_k,
           lnx_w, lnx_b, w_sb_out, w_rw_out, w_o, norm_mix_post, norm_ffn_pre,
           w_ffn_gate, w_ffn_up, w_ffn_down, norm_ffn_post):
    B, S, D = x.shape
    T = B * S
    NS = B * HEADS
    assert D == D_MODEL and S % SB_TILE == 0 and NS % LANES == 0
    tm = min(512, S)
    row = lambda a: a.reshape(1, -1)

    xf = x.reshape(T, D)
    w_in_b = w_in.astype(BF16)
    wwa = jnp.zeros((W_LORA + A_LORA, 2 * WIDTH), F32)
    wwa = wwa.at[:W_LORA, :WIDTH].set(w_up).at[W_LORA:, WIDTH:].set(a_up)
    qkv, rw, g = _in_proj(xf, row(norm_mix_pre), w_in_b[:, :MIX_COLS], row(mu_rw), row(w0),
                          row(a0), wwa, g_up, S, tm)

    o_sb = _sb_attn(qkv.reshape(B, S, SB_COLS))

    lane_head = 2 * (jnp.arange(LANES) % (HEADS // 2)) + jnp.arange(LANES) // HEAD_DIM
    per_stream = lambda a: a.reshape(HEADS, HEAD_DIM)[lane_head].T
    y = _wkv(rw.reshape(RW_STREAMS, S, B * WIDTH // LANES, LANES), per_stream(lnx_w),
             per_stream(lnx_b), per_stream(r_k), per_stream(k_k), per_stream(k_a), min(64, S))
    y = y.reshape(S, B * WIDTH)

    x1 = _merge(xf, o_sb.reshape(T, WIDTH), y, g, row(norm_mix_pre),
                w_in_b[:, MIX_COLS:], row(b_gate), w_sb_out.astype(BF16),
                w_rw_out.astype(BF16), w_o.astype(BF16), row(norm_mix_post), tm)
    out = _ffn(x1, row(norm_ffn_pre), w_ffn_gate.astype(BF16), w_ffn_up.astype(BF16),
               w_ffn_down.astype(BF16), row(norm_ffn_post), tm)
    return out.reshape(B, S, D)


def kernel(x, norm_mix_pre, w_in, b_gate, mu_rw, w0, w_up, a0, a_up, g_up, k_k, k_a, r_k,
           lnx_w, lnx_b, w_sb_out, w_rw_out, w_o, norm_mix_post, norm_ffn_pre,
           w_ffn_gate, w_ffn_up, w_ffn_down, norm_ffn_post):
    params = (norm_mix_pre, w_in, b_gate, mu_rw, w0, w_up, a0, a_up, g_up, k_k, k_a, r_k,
              lnx_w, lnx_b, w_sb_out, w_rw_out, w_o, norm_mix_post, norm_ffn_pre,
              w_ffn_gate, w_ffn_up, w_ffn_down, norm_ffn_post)
    for l in range(norm_mix_pre.shape[0]):
        x = _layer(x, *(p[l] for p in params))
    return x
```

```python
import functools

import jax
import jax.numpy as jnp
from jax import lax
from jax.experimental import pallas as pl
from jax.experimental.pallas import tpu as pltpu

F32 = jnp.float32
BF16 = jnp.bfloat16

D_MODEL = 1024
HEADS = 8
HEAD_DIM = 64
WIDTH = HEADS * HEAD_DIM
W_LORA = 64
A_LORA = 64
G_LORA = 128
SB_COLS = 3 * WIDTH
RW_COLS = 3 * WIDTH + W_LORA + A_LORA + G_LORA
MIX_COLS = SB_COLS + RW_COLS
D_FF = 2816
RW_STREAMS = 5
NVB = HEAD_DIM // 8
VB_GROUP = 2
N_CHAINS = 4
RW_ROWS = 256
RMS_EPS = 1e-6
GN_EPS = HEAD_DIM * 1e-5
SB_TILE = 256
SB_SLOTS = 3
LOG2E = 1.4426950408889634
LANES = 128
SUBLANES = 8
MXU_N = 256
VMEM_LIMIT = 56 * 1024 * 1024


def _rms(x, g):
    ms = jnp.mean(x * x, axis=-1, keepdims=True)
    return x * lax.rsqrt(ms + RMS_EPS) * g


def _sigmoid(x):
    return 1.0 / (1.0 + jnp.exp(-x))


def _neg_abs(x):
    if x.dtype.itemsize != 4:
        return -jnp.abs(x)
    bits = lax.bitcast_convert_type(x, jnp.uint32) | jnp.uint32(0x80000000)
    return lax.bitcast_convert_type(bits, x.dtype)


def _tree_sum(xs):
    while len(xs) > 1:
        xs = [a + b for a, b in zip(xs[::2], xs[1::2])] + ([xs[-1]] if len(xs) % 2 else [])
    return xs[0]


def _softplus(x):
    return jnp.maximum(x, 0.0) + jnp.log(1.0 + jnp.exp(-jnp.abs(x)))


def _dot3(x, w_hi, w_lo):
    x_hi = x.astype(BF16)
    x_lo = (x - x_hi.astype(F32)).astype(BF16)
    return (jnp.dot(x_hi, w_hi, preferred_element_type=F32)
            + jnp.dot(x_lo, w_hi, preferred_element_type=F32)
            + jnp.dot(x_hi, w_lo, preferred_element_type=F32))


def _in_proj_kernel(tiles_per_seq, x_ref, g_ref, w_ref, mu_ref, w0_ref, a0_ref, wwa_hi_ref,
                    wwa_lo_ref, gup_hi_ref, gup_lo_ref, qkv_ref, rw_ref, gate_ref,
                    h_scr, p_scr, last_scr):
    tm = x_ref.shape[0]
    h_scr[...] = _rms(x_ref[...], g_ref[...]).astype(BF16)

    for c0 in list(range(SB_COLS, MIX_COLS, MXU_N)) + list(range(0, SB_COLS, MXU_N)):
        acc = jnp.dot(h_scr[...], w_ref[:, c0:c0 + MXU_N], preferred_element_type=F32)
        if c0 < WIDTH:
            acc = acc * (LOG2E * HEAD_DIM ** -0.5)
        if c0 < SB_COLS:
            qkv_ref[:, c0:c0 + MXU_N] = acc.astype(BF16)
        else:
            p_scr[:, c0 - SB_COLS:c0 - SB_COLS + MXU_N] = acc

    seq_start = pl.program_id(0) % tiles_per_seq == 0
    rowid = lax.broadcasted_iota(jnp.int32, (RW_ROWS, RW_COLS), 0)
    lane = lax.broadcasted_iota(jnp.int32, (RW_ROWS, W_LORA + A_LORA), 1)
    for r0 in range(0, tm, RW_ROWS):
        cur = p_scr[r0:r0 + RW_ROWS]
        if r0 == 0:
            before = jnp.where(seq_start, 0.0, last_scr[SUBLANES - 1:SUBLANES])
        else:
            before = p_scr[r0 - 1:r0]
        prev = jnp.where(rowid == 0, before, pltpu.roll(cur, 1, axis=0))
        p = cur + (prev - cur) * mu_ref[...]
        xwa = p[:, 3 * WIDTH:3 * WIDTH + W_LORA + A_LORA]
        xg = p[:, 3 * WIDTH + W_LORA + A_LORA:]
        twa = jnp.where(lane < W_LORA, jnp.tanh(xwa), xwa)
        wa = _dot3(twa, wwa_hi_ref[...], wwa_lo_ref[...])
        w_raw = w0_ref[...] + wa[:, :WIDTH]
        rs = slice(r0, r0 + RW_ROWS)
        rw_ref[0, rs] = p[:, 0:WIDTH]
        rw_ref[1, rs] = -jnp.exp(-_softplus(-w_raw) - 0.5)
        rw_ref[2, rs] = p[:, WIDTH:2 * WIDTH]
        rw_ref[3, rs] = p[:, 2 * WIDTH:3 * WIDTH]
        rw_ref[4, rs] = _sigmoid(a0_ref[...] + wa[:, WIDTH:])
        gate_ref[rs] = _dot3(_sigmoid(xg), gup_hi_ref[...], gup_lo_ref[...]).astype(BF16)
    last_scr[...] = p_scr[tm - SUBLANES:tm]


def _in_proj(xf, g, w, mu, w0, a0, wwa, g_up, seq_len, tm):
    T = xf.shape[0]
    spb = seq_len // tm
    hi = lambda a: a.astype(BF16)
    lo = lambda a: (a - a.astype(BF16).astype(F32)).astype(BF16)
    full = lambda a: pl.BlockSpec(a.shape, lambda i: (0, 0))
    consts = (g, w, mu, w0, a0, hi(wwa), lo(wwa), hi(g_up), lo(g_up))
    return pl.pallas_call(
        functools.partial(_in_proj_kernel, spb),
        grid=(T // tm,),
        in_specs=[pl.BlockSpec((tm, D_MODEL), lambda i: (i, 0))] + [full(a) for a in consts],
        out_specs=[
            pl.BlockSpec((tm, SB_COLS), lambda i: (i, 0)),
            pl.BlockSpec((RW_STREAMS, tm, WIDTH), lambda i: (0, i % spb, i // spb)),
            pl.BlockSpec((tm, WIDTH), lambda i: (i, 0)),
        ],
        out_shape=[
            jax.ShapeDtypeStruct((T, SB_COLS), BF16),
            jax.ShapeDtypeStruct((RW_STREAMS, seq_len, (T // seq_len) * WIDTH), F32),
            jax.ShapeDtypeStruct((T, WIDTH), BF16),
        ],
        scratch_shapes=[pltpu.VMEM((tm, D_MODEL), BF16), pltpu.VMEM((tm, RW_COLS), F32),
                        pltpu.VMEM((SUBLANES, RW_COLS), F32)],
        compiler_params=pltpu.CompilerParams(
            dimension_semantics=("arbitrary",), vmem_limit_bytes=VMEM_LIMIT),
        name="in_proj",
    )(xf, *consts)


def _sb_kernel(ti_ref, tj_ref, q_ref, k_ref, v_ref, o_ref, z_scr, c_scr, acc_scr, car_scr):
    S = q_ref.shape[1]
    nq = S // SB_TILE
    n_off = nq * (nq - 1) // 2
    row = lax.broadcasted_iota(jnp.int32, (SB_TILE, SB_TILE), 0)
    col = lax.broadcasted_iota(jnp.int32, (SB_TILE, SB_TILE), 1)
    incl_mat = jnp.where(row >= col, 1.0, 0.0).astype(BF16)
    strict = col < row
    head0 = lax.broadcasted_iota(jnp.int32, (SB_TILE, LANES), 1) < HEAD_DIM

    def rows(ref, t):
        return ref[0, pl.ds(pl.multiple_of(t * SB_TILE, SB_TILE), SB_TILE), :]

    def issue_logits(i, j, slot):
        q = rows(q_ref, i)
        kb = rows(k_ref, j)
        for h in range(2):
            qh = jnp.where(head0 if h == 0 else ~head0, q, jnp.zeros_like(q))
            z_scr[slot, h] = lax.dot_general(qh, kb, (((1,), (1,)), ((), ())),
                                             preferred_element_type=F32)

    def cumulate(slot, diag):
        for h in range(2):
            z = z_scr[slot, h]
            s = jnp.maximum(z, 0.0) + jnp.log(1.0 + jnp.exp2(_neg_abs(z))) * LOG2E
            if diag:
                s = jnp.where(strict, s, 0.0)
            c_scr[slot, h] = jnp.dot(s.astype(BF16), incl_mat, preferred_element_type=F32)

    def finish(i, j, slot, diag):
        vb = rows(v_ref, j)
        for h in range(2):
            c = c_scr[slot, h]
            e = z_scr[slot, h] - c
            if not diag:
                car = car_scr[i, h]
                e = e - jnp.concatenate([car, car], axis=1)
            w = jnp.exp2(e)
            if diag:
                w = jnp.where(strict, w, 0.0)
            pv = jnp.dot(w.astype(BF16), vb, preferred_element_type=F32)
            tot = jnp.broadcast_to(c[:, 0:1], (SB_TILE, LANES))
            if diag:
                acc_scr[i, h] = pv
                car_scr[i, h] = tot
            else:
                acc_scr[i, h] += pv
                car_scr[i, h] += tot

    def run(steps, tile_of, diag):
        def step(n, slot):
            issue_logits(*tile_of(jnp.minimum(n + 1, steps - 1)), (slot + 1) % SB_SLOTS)
            finish(*tile_of(n - 1), (slot - 1) % SB_SLOTS, diag)
            cumulate(slot, diag)

        issue_logits(*tile_of(0), 0)
        issue_logits(*tile_of(min(1, steps - 1)), 1)
        cumulate(0, diag)
        groups = (steps - 1) // SB_SLOTS

        def body(g, _):
            for u in range(SB_SLOTS):
                step(1 + SB_SLOTS * g + u, (1 + u) % SB_SLOTS)
            return 0

        lax.fori_loop(0, groups, body, 0)
        for n in range(1 + SB_SLOTS * groups, steps):
            step(n, n % SB_SLOTS)
        finish(*tile_of(steps - 1), (steps - 1) % SB_SLOTS, diag)

    run(nq, lambda n: (n, n), True)
    if n_off:
        run(n_off, lambda n: (ti_ref[n], tj_ref[n]), False)
    for i in range(nq):
        o_ref[0, i * SB_TILE:(i + 1) * SB_TILE, :] = jnp.where(
            head0, acc_scr[i, 0], acc_scr[i, 1]).astype(o_ref.dtype)


def _sb_attn(qkv):
    B, S, _ = qkv.shape
    npair = WIDTH // LANES
    nq = S // SB_TILE
    pairs = [(i, j) for i in range(nq) for j in range(i - 1, -1, -1)] or [(0, 0)]
    ti = jnp.array([p[0] for p in pairs], jnp.int32)
    tj = jnp.array([p[1] for p in pairs], jnp.int32)
    return pl.pallas_call(
        _sb_kernel,
        grid_spec=pltpu.PrefetchScalarGridSpec(
            num_scalar_prefetch=2,
            grid=(B, npair),
            in_specs=[
                pl.BlockSpec((1, S, LANES), lambda b, p, ti, tj: (b, 0, p)),
                pl.BlockSpec((1, S, LANES), lambda b, p, ti, tj: (b, 0, npair + p)),
                pl.BlockSpec((1, S, LANES), lambda b, p, ti, tj: (b, 0, 2 * npair + p)),
            ],
            out_specs=pl.BlockSpec((1, S, LANES), lambda b, p, ti, tj: (b, 0, p)),
            scratch_shapes=[
                pltpu.VMEM((SB_SLOTS, 2, SB_TILE, SB_TILE), F32),
                pltpu.VMEM((SB_SLOTS, 2, SB_TILE, SB_TILE), F32),
                pltpu.VMEM((nq, 2, SB_TILE, LANES), F32),
                pltpu.VMEM((nq, 2, SB_TILE, LANES), F32),
            ],
        ),
        out_shape=jax.ShapeDtypeStruct((B, S, WIDTH), BF16),
        compiler_params=pltpu.CompilerParams(
            dimension_semantics=("parallel", "parallel"), vmem_limit_bytes=VMEM_LIMIT),
        name="sb_attn",
    )(ti, tj, qkv, qkv, qkv)


def _wkv_kernel(r_ref, w_ref, k_ref, v_ref, a_ref, lw_ref, lb_ref, rk_ref, kk_ref, ka_ref, o_ref,
                st_ref, y_scr, x_scr, o_scr, ld_scr, vec_scr):
    @pl.when(pl.program_id(1) == 0)
    def _():
        st_ref[...] = jnp.zeros_like(st_ref)

    ld_scr[...] = jnp.zeros_like(ld_scr)
    tc = r_ref.shape[0]

    def to_streams(t):
        for n, ref in enumerate((r_ref, w_ref, k_ref, v_ref, a_ref)):
            x_scr[n, t] = ref[t].T

    def to_tokens(t):
        o_ref[t] = o_scr[t].T

    def step(t, g):
        rows = slice(g * HEAD_DIM, (g + 1) * HEAD_DIM)
        a = x_scr[4, t, rows]
        k = x_scr[2, t, rows]
        kk = k * kk_ref[g]
        k = k * (1.0 + (a - 1.0) * ka_ref[g])
        norm = jnp.sqrt(jnp.sum(kk * kk, axis=0, keepdims=True))
        kkn = kk * (1.0 / jnp.maximum(norm, 1e-12))
        r = x_scr[0, t, rows]
        vt = x_scr[3, t, rows]
        g_before = jnp.exp(ld_scr[g])
        ld = ld_scr[g] + x_scr[1, t, rows]
        ld_scr[g] = ld
        g_now = jnp.exp(ld)
        g_inv = jnp.exp(-ld)
        vec_scr[g, 0] = -kkn * g_before
        vec_scr[g, 1] = kkn * a * g_inv
        vec_scr[g, 2] = k * g_inv
        vec_scr[g, 3] = r * g_now
        chan = lambda i, kc: vec_scr[g, i, kc:kc + 1, :]
        for vb0 in range(0, NVB, VB_GROUP):
            blocks = range(vb0, vb0 + VB_GROUP)
            acc = {(b, c): None for b in blocks for c in range(N_CHAINS)}
            for kc in range(HEAD_DIM):
                nrow = chan(0, kc)
                for b in blocks:
                    key = (b, kc % N_CHAINS)
                    prod = st_ref[g, b, kc] * nrow
                    acc[key] = prod if acc[key] is None else acc[key] + prod
            sa = {b: _tree_sum([acc[(b, c)] for c in range(N_CHAINS)]) for b in blocks}
            vrow = {b: vt[b * SUBLANES:(b + 1) * SUBLANES] for b in blocks}
            acc = {(b, c): None for b in blocks for c in range(N_CHAINS)}
            for kc in range(HEAD_DIM):
                brow, krow, rrow = chan(1, kc), chan(2, kc), chan(3, kc)
                for b in blocks:
                    key = (b, kc % N_CHAINS)
                    s = st_ref[g, b, kc] + sa[b] * brow + vrow[b] * krow
                    st_ref[g, b, kc] = s
                    prod = s * rrow
                    acc[key] = prod if acc[key] is None else acc[key] + prod
            for b in blocks:
                y_scr[g, b * SUBLANES:(b + 1) * SUBLANES, :] = _tree_sum(
                    [acc[(b, c)] for c in range(N_CHAINS)])
        y = y_scr[g]
        d = y - jnp.mean(y, axis=0, keepdims=True)
        var = jnp.mean(d * d, axis=0, keepdims=True)
        bonus = jnp.sum(r * k * rk_ref[g], axis=0, keepdims=True) * vt
        o_scr[t, rows] = d * lax.rsqrt(var + GN_EPS) * lw_ref[g] + lb_ref[g] + bonus

    def advance(t):
        for g in range(2):
            step(t, g)
        to_streams(jnp.minimum(t + 1, tc - 1))

    to_streams(0)
    advance(0)

    def body(t, _):
        to_tokens(t - 1)
        advance(t)
        return 0

    lax.fori_loop(1, tc, body, 0)
    to_tokens(tc - 1)
    for g in range(2):
        vec_scr[g, 0] = jnp.exp(ld_scr[g])
        for vb in range(NVB):
            for kc in range(HEAD_DIM):
                st_ref[g, vb, kc] = st_ref[g, vb, kc] * vec_scr[g, 0, kc:kc + 1, :]


def _wkv(rw, lw, lb, rk, kkc, kac, tc):
    _, S, R, _ = rw.shape
    N = HEAD_DIM
    inp = lambda n: pl.BlockSpec((None, tc, LANES, LANES), lambda g, c: (n, c, g, 0))
    const = pl.BlockSpec((2, N, LANES), lambda g, c: (0, 0, 0))
    return pl.pallas_call(
        _wkv_kernel,
        grid=(R // LANES, S // tc),
        in_specs=[inp(n) for n in range(RW_STREAMS)] + [const] * 5,
        out_specs=pl.BlockSpec((tc, LANES, LANES), lambda g, c: (c, g, 0)),
        out_shape=jax.ShapeDtypeStruct((S, R, LANES), F32),
        scratch_shapes=[pltpu.VMEM((2, NVB, N, SUBLANES, LANES), F32),
                        pltpu.VMEM((2, N, LANES), F32),
                        pltpu.VMEM((RW_STREAMS, tc, LANES, LANES), F32),
                        pltpu.VMEM((tc, LANES, LANES), F32), pltpu.VMEM((2, N, LANES), F32),
                        pltpu.VMEM((2, 4, N, LANES), F32)],
        compiler_params=pltpu.CompilerParams(
            dimension_semantics=("parallel", "arbitrary"), vmem_limit_bytes=VMEM_LIMIT),
        name="wkv",
    )(*([rw] * RW_STREAMS), lw, lb, rk, kkc, kac)


def _merge_kernel(x_ref, osb_ref, y_ref, g_ref, gpre_ref, wg_ref, bg_ref, wsb_ref, wrw_ref,
                  wo_ref, gpost_ref, x1_ref, h_scr, orw_scr, m_scr, f_scr):
    x = x_ref[...]
    h_scr[...] = _rms(x, gpre_ref[...]).astype(BF16)
    orw_scr[...] = (y_ref[...] * g_ref[...].astype(F32)).astype(BF16)
    for c0 in range(0, D_MODEL, MXU_N):
        cs = slice(c0, c0 + MXU_N)
        cr = slice(D_MODEL + c0, D_MODEL + c0 + MXU_N)
        g_sb = _sigmoid(jnp.dot(h_scr[...], wg_ref[:, cs], preferred_element_type=F32)
                        + bg_ref[:, cs])
        g_rw = _sigmoid(jnp.dot(h_scr[...], wg_ref[:, cr], preferred_element_type=F32)
                        + bg_ref[:, cr])
        a_sb = jnp.dot(osb_ref[...], wsb_ref[:, cs], preferred_element_type=F32)
        a_rw = jnp.dot(orw_scr[...], wrw_ref[:, cs], preferred_element_type=F32)
        m_scr[:, cs] = (g_sb * a_sb + g_rw * a_rw).astype(BF16)
    for c0 in range(0, D_MODEL, MXU_N):
        cs = slice(c0, c0 + MXU_N)
        f_scr[:, cs] = jnp.dot(m_scr[...], wo_ref[:, cs], preferred_element_type=F32)
    x1_ref[...] = x + _rms(f_scr[...], gpost_ref[...])


def _merge(xf, o_sb, y, g, gpre, wg, bg, wsb, wrw, wo, gpost, tm):
    T = xf.shape[0]
    spb = y.shape[0] // tm
    tile = lambda c: pl.BlockSpec((tm, c), lambda i: (i, 0))
    y_tile = pl.BlockSpec((tm, WIDTH), lambda i: (i % spb, i // spb))
    full = lambda a: pl.BlockSpec(a.shape, lambda i: (0, 0))
    return pl.pallas_call(
        _merge_kernel,
        grid=(T // tm,),
        in_specs=[tile(D_MODEL), tile(WIDTH), y_tile, tile(WIDTH), full(gpre), full(wg),
                  full(bg), full(wsb), full(wrw), full(wo), full(gpost)],
        out_specs=tile(D_MODEL),
        out_shape=jax.ShapeDtypeStruct((T, D_MODEL), F32),
        scratch_shapes=[pltpu.VMEM((tm, D_MODEL), BF16), pltpu.VMEM((tm, WIDTH), BF16),
                        pltpu.VMEM((tm, D_MODEL), BF16), pltpu.VMEM((tm, D_MODEL), F32)],
        compiler_params=pltpu.CompilerParams(
            dimension_semantics=("parallel",), vmem_limit_bytes=VMEM_LIMIT),
        name="merge",
    )(xf, o_sb, y, g, gpre, wg, bg, wsb, wrw, wo, gpost)


def _ffn_kernel(x_ref, gpre_ref, wg_ref, wu_ref, wd_ref, gpost_ref, o_ref, h_scr, acc_scr):
    x = x_ref[...]
    h_scr[...] = _rms(x, gpre_ref[...]).astype(BF16)
    for c0 in range(0, D_FF, MXU_N):
        cs = slice(c0, c0 + MXU_N)
        gt = jnp.dot(h_scr[...], wg_ref[:, cs], preferred_element_type=F32)
        up = jnp.dot(h_scr[...], wu_ref[:, cs], preferred_element_type=F32)
        act = (gt * _sigmoid(gt) * up).astype(BF16)
        part = jnp.dot(act, wd_ref[cs, :], preferred_element_type=F32)
        if c0 == 0:
            acc_scr[...] = part
        else:
            acc_scr[...] += part
    o_ref[...] = x + _rms(acc_scr[...], gpost_ref[...])


def _ffn(x1, gpre, wg, wu, wd, gpost, tm):
    T = x1.shape[0]
    tile = pl.BlockSpec((tm, D_MODEL), lambda i: (i, 0))
    full = lambda a: pl.BlockSpec(a.shape, lambda i: (0, 0))
    return pl.pallas_call(
        _ffn_kernel,
        grid=(T // tm,),
        in_specs=[tile, full(gpre), full(wg), full(wu), full(wd), full(gpost)],
        out_specs=tile,
        out_shape=jax.ShapeDtypeStruct((T, D_MODEL), F32),
        scratch_shapes=[pltpu.VMEM((tm, D_MODEL), BF16), pltpu.VMEM((tm, D_MODEL), F32)],
        compiler_params=pltpu.CompilerParams(
            dimension_semantics=("parallel",), vmem_limit_bytes=VMEM_LIMIT),
        name="ffn",
    )(x1, gpre, wg, wu, wd, gpost)


def _layer(x, norm_mix_pre, w_in, b_gate, mu_rw, w0, w_up, a0, a_up, g_up, k_k, k_a, r_k,
           lnx_w, lnx_b, w_sb_out, w_rw_out, w_o, norm_mix_post, norm_ffn_pre,
           w_ffn_gate, w_ffn_up, w_ffn_down, norm_ffn_post):
    B, S, D = x.shape
    T = B * S
    NS = B * HEADS
    assert D == D_MODEL and S % SB_TILE == 0 and (NS // 2) % LANES == 0
    tm = min(512, S)
    row = lambda a: a.reshape(1, -1)

    xf = x.reshape(T, D)
    w_in_b = w_in.astype(BF16)
    wwa = jnp.zeros((W_LORA + A_LORA, 2 * WIDTH), F32)
    wwa = wwa.at[:W_LORA, :WIDTH].set(w_up).at[W_LORA:, WIDTH:].set(a_up)
    qkv, rw, g = _in_proj(xf, row(norm_mix_pre), w_in_b[:, :MIX_COLS], row(mu_rw), row(w0),
                          row(a0), wwa, g_up, S, tm)

    o_sb = _sb_attn(qkv.reshape(B, S, SB_COLS))

    lane_head = jnp.stack([2 * (jnp.arange(LANES) % (HEADS // 2)) + p for p in range(2)])
    per_stream = lambda a: a.reshape(HEADS, HEAD_DIM)[lane_head].transpose(0, 2, 1)
    y = _wkv(rw.reshape(RW_STREAMS, S, B * WIDTH // LANES, LANES), per_stream(lnx_w),
             per_stream(lnx_b), per_stream(r_k), per_stream(k_k), per_stream(k_a), min(32, S))
    y = y.reshape(S, B * WIDTH)

    x1 = _merge(xf, o_sb.reshape(T, WIDTH), y, g, row(norm_mix_pre),
                w_in_b[:, MIX_COLS:], row(b_gate), w_sb_out.astype(BF16),
                w_rw_out.astype(BF16), w_o.astype(BF16), row(norm_mix_post), tm)
    out = _ffn(x1, row(norm_ffn_pre), w_ffn_gate.astype(BF16), w_ffn_up.astype(BF16),
               w_ffn_down.astype(BF16), row(norm_ffn_post), tm)
    return out.reshape(B, S, D)


def kernel(x, norm_mix_pre, w_in, b_gate, mu_rw, w0, w_up, a0, a_up, g_up, k_k, k_a, r_k,
           lnx_w, lnx_b, w_sb_out, w_rw_out, w_o, norm_mix_post, norm_ffn_pre,
           w_ffn_gate, w_ffn_up, w_ffn_down, norm_ffn_post):
    params = (norm_mix_pre, w_in, b_gate, mu_rw, w0, w_up, a0, a_up, g_up, k_k, k_a, r_k,
              lnx_w, lnx_b, w_sb_out, w_rw_out, w_o, norm_mix_post, norm_ffn_pre,
              w_ffn_gate, w_ffn_up, w_ffn_down, norm_ffn_post)
    for l in range(norm_mix_pre.shape[0]):
        x = _layer(x, *(p[l] for p in params))
    return x
```

```python
import functools

import jax
import jax.numpy as jnp
from jax import lax
from jax.experimental import pallas as pl
from jax.experimental.pallas import tpu as pltpu

F32 = jnp.float32
BF16 = jnp.bfloat16

D_MODEL = 1024
HEADS = 8
HEAD_DIM = 64
WIDTH = HEADS * HEAD_DIM
W_LORA = 64
A_LORA = 64
G_LORA = 128
SB_COLS = 3 * WIDTH
RW_COLS = 3 * WIDTH + W_LORA + A_LORA + G_LORA
MIX_COLS = SB_COLS + RW_COLS
D_FF = 2816
RMS_EPS = 1e-6
GN_EPS = HEAD_DIM * 1e-5
LOG2E = 1.4426950408889634

LANES = 128
SUBLANES = 8
MXU_N = 256
VMEM_LIMIT = 56 * 1024 * 1024

TOKEN_TILE = 512
RW_STREAMS = 5
RW_ROWS = 256
SB_TILE = 256
SB_SLOTS = 3
WKV_BLOCK = 32
NVB = HEAD_DIM // SUBLANES
VB_GROUP = 4
N_CHAINS = 4


def _rms(x, g):
    ms = jnp.mean(x * x, axis=-1, keepdims=True)
    return x * lax.rsqrt(ms + RMS_EPS) * g


def _sigmoid(x):
    return 1.0 / (1.0 + jnp.exp(-x))


def _neg_abs(x):
    if x.dtype.itemsize != 4:
        return -jnp.abs(x)
    bits = lax.bitcast_convert_type(x, jnp.uint32) | jnp.uint32(0x80000000)
    return lax.bitcast_convert_type(bits, x.dtype)


def _tree_sum(xs):
    while len(xs) > 1:
        xs = [a + b for a, b in zip(xs[::2], xs[1::2])] + ([xs[-1]] if len(xs) % 2 else [])
    return xs[0]


def _softplus(x):
    return jnp.maximum(x, 0.0) + jnp.log(1.0 + jnp.exp(-jnp.abs(x)))


def _dot3(x, w_hi, w_lo):
    x_hi = x.astype(BF16)
    x_lo = (x - x_hi.astype(F32)).astype(BF16)
    return (jnp.dot(x_hi, w_hi, preferred_element_type=F32)
            + jnp.dot(x_lo, w_hi, preferred_element_type=F32)
            + jnp.dot(x_hi, w_lo, preferred_element_type=F32))


def _in_proj_kernel(tiles_per_seq, x_ref, g_ref, w_ref, mu_ref, w0_ref, a0_ref, wwa_hi_ref,
                    wwa_lo_ref, gup_hi_ref, gup_lo_ref, qkv_ref, rw_ref, gate_ref,
                    h_scr, p_scr, last_scr):
    tm = x_ref.shape[0]
    h_scr[...] = _rms(x_ref[...], g_ref[...]).astype(BF16)

    for c0 in list(range(SB_COLS, MIX_COLS, MXU_N)) + list(range(0, SB_COLS, MXU_N)):
        acc = jnp.dot(h_scr[...], w_ref[:, c0:c0 + MXU_N], preferred_element_type=F32)
        if c0 < WIDTH:
            acc = acc * (LOG2E * HEAD_DIM ** -0.5)
        if c0 < SB_COLS:
            qkv_ref[:, c0:c0 + MXU_N] = acc.astype(BF16)
        else:
            p_scr[:, c0 - SB_COLS:c0 - SB_COLS + MXU_N] = acc

    seq_start = pl.program_id(0) % tiles_per_seq == 0
    rowid = lax.broadcasted_iota(jnp.int32, (RW_ROWS, RW_COLS), 0)
    lane = lax.broadcasted_iota(jnp.int32, (RW_ROWS, W_LORA + A_LORA), 1)
    for r0 in range(0, tm, RW_ROWS):
        cur = p_scr[r0:r0 + RW_ROWS]
        if r0 == 0:
            before = jnp.where(seq_start, 0.0, last_scr[SUBLANES - 1:SUBLANES])
        else:
            before = p_scr[r0 - 1:r0]
        prev = jnp.where(rowid == 0, before, pltpu.roll(cur, 1, axis=0))
        p = cur + (prev - cur) * mu_ref[...]
        xwa = p[:, 3 * WIDTH:3 * WIDTH + W_LORA + A_LORA]
        xg = p[:, 3 * WIDTH + W_LORA + A_LORA:]
        twa = jnp.where(lane < W_LORA, jnp.tanh(xwa), xwa)
        wa = _dot3(twa, wwa_hi_ref[...], wwa_lo_ref[...])
        w_raw = w0_ref[...] + wa[:, :WIDTH]
        rs = slice(r0, r0 + RW_ROWS)
        rw_ref[0, rs] = p[:, 0:WIDTH]
        rw_ref[1, rs] = -jnp.exp(-_softplus(-w_raw) - 0.5)
        rw_ref[2, rs] = p[:, WIDTH:2 * WIDTH]
        rw_ref[3, rs] = p[:, 2 * WIDTH:3 * WIDTH]
        rw_ref[4, rs] = _sigmoid(a0_ref[...] + wa[:, WIDTH:])
        gate_ref[rs] = _dot3(_sigmoid(xg), gup_hi_ref[...], gup_lo_ref[...]).astype(BF16)
    last_scr[...] = p_scr[tm - SUBLANES:tm]


def _in_proj(xf, g, w, mu, w0, a0, wwa, g_up, seq_len, tm):
    T = xf.shape[0]
    spb = seq_len // tm
    hi = lambda a: a.astype(BF16)
    lo = lambda a: (a - a.astype(BF16).astype(F32)).astype(BF16)
    full = lambda a: pl.BlockSpec(a.shape, lambda i: (0, 0))
    consts = (g, w, mu, w0, a0, hi(wwa), lo(wwa), hi(g_up), lo(g_up))
    return pl.pallas_call(
        functools.partial(_in_proj_kernel, spb),
        grid=(T // tm,),
        in_specs=[pl.BlockSpec((tm, D_MODEL), lambda i: (i, 0))] + [full(a) for a in consts],
        out_specs=[
            pl.BlockSpec((tm, SB_COLS), lambda i: (i, 0)),
            pl.BlockSpec((RW_STREAMS, tm, WIDTH), lambda i: (0, i % spb, i // spb)),
            pl.BlockSpec((tm, WIDTH), lambda i: (i, 0)),
        ],
        out_shape=[
            jax.ShapeDtypeStruct((T, SB_COLS), BF16),
            jax.ShapeDtypeStruct((RW_STREAMS, seq_len, (T // seq_len) * WIDTH), F32),
            jax.ShapeDtypeStruct((T, WIDTH), BF16),
        ],
        scratch_shapes=[pltpu.VMEM((tm, D_MODEL), BF16), pltpu.VMEM((tm, RW_COLS), F32),
                        pltpu.VMEM((SUBLANES, RW_COLS), F32)],
        compiler_params=pltpu.CompilerParams(
            dimension_semantics=("arbitrary",), vmem_limit_bytes=VMEM_LIMIT),
        name="in_proj",
    )(xf, *consts)


def _sb_kernel(ti_ref, tj_ref, q_ref, k_ref, v_ref, o_ref, z_scr, c_scr, acc_scr, car_scr):
    S = q_ref.shape[1]
    nq = S // SB_TILE
    n_off = nq * (nq - 1) // 2
    row = lax.broadcasted_iota(jnp.int32, (SB_TILE, SB_TILE), 0)
    col = lax.broadcasted_iota(jnp.int32, (SB_TILE, SB_TILE), 1)
    incl_mat = jnp.where(row >= col, 1.0, 0.0).astype(BF16)
    strict = col < row
    head0 = lax.broadcasted_iota(jnp.int32, (SB_TILE, LANES), 1) < HEAD_DIM

    def rows(ref, t):
        return ref[0, pl.ds(pl.multiple_of(t * SB_TILE, SB_TILE), SB_TILE), :]

    def issue_logits(i, j, slot):
        q = rows(q_ref, i)
        kb = rows(k_ref, j)
        for h in range(2):
            qh = jnp.where(head0 if h == 0 else ~head0, q, jnp.zeros_like(q))
            z_scr[slot, h] = lax.dot_general(qh, kb, (((1,), (1,)), ((), ())),
                                             preferred_element_type=F32)

    def cumulate(slot, diag):
        for h in range(2):
            z = z_scr[slot, h]
            s = jnp.maximum(z, 0.0) + jnp.log(1.0 + jnp.exp2(_neg_abs(z))) * LOG2E
            if diag:
                s = jnp.where(strict, s, 0.0)
            c_scr[slot, h] = jnp.dot(s.astype(BF16), incl_mat, preferred_element_type=F32)

    def finish(i, j, slot, diag):
        vb = rows(v_ref, j)
        for h in range(2):
            c = c_scr[slot, h]
            e = z_scr[slot, h] - c
            if not diag:
                car = car_scr[i, h]
                e = e - jnp.concatenate([car, car], axis=1)
            w = jnp.exp2(e)
            if diag:
                w = jnp.where(strict, w, 0.0)
            pv = jnp.dot(w.astype(BF16), vb, preferred_element_type=F32)
            tot = jnp.broadcast_to(c[:, 0:1], (SB_TILE, LANES))
            if diag:
                acc_scr[i, h] = pv
                car_scr[i, h] = tot
            else:
                acc_scr[i, h] += pv
                car_scr[i, h] += tot

    def run(steps, tile_of, diag):
        def step(n, slot):
            issue_logits(*tile_of(jnp.minimum(n + 1, steps - 1)), (slot + 1) % SB_SLOTS)
            finish(*tile_of(n - 1), (slot - 1) % SB_SLOTS, diag)
            cumulate(slot, diag)

        issue_logits(*tile_of(0), 0)
        issue_logits(*tile_of(min(1, steps - 1)), 1)
        cumulate(0, diag)
        groups = (steps - 1) // SB_SLOTS

        def body(g, _):
            for u in range(SB_SLOTS):
                step(1 + SB_SLOTS * g + u, (1 + u) % SB_SLOTS)
            return 0

        lax.fori_loop(0, groups, body, 0)
        for n in range(1 + SB_SLOTS * groups, steps):
            step(n, n % SB_SLOTS)
        finish(*tile_of(steps - 1), (steps - 1) % SB_SLOTS, diag)

    run(nq, lambda n: (n, n), True)
    if n_off:
        run(n_off, lambda n: (ti_ref[n], tj_ref[n]), False)
    for i in range(nq):
        o_ref[0, i * SB_TILE:(i + 1) * SB_TILE, :] = jnp.where(
            head0, acc_scr[i, 0], acc_scr[i, 1]).astype(o_ref.dtype)


def _sb_attn(qkv):
    B, S, _ = qkv.shape
    npair = WIDTH // LANES
    nq = S // SB_TILE
    pairs = [(i, j) for i in range(nq) for j in range(i - 1, -1, -1)] or [(0, 0)]
    ti = jnp.array([p[0] for p in pairs], jnp.int32)
    tj = jnp.array([p[1] for p in pairs], jnp.int32)
    return pl.pallas_call(
        _sb_kernel,
        grid_spec=pltpu.PrefetchScalarGridSpec(
            num_scalar_prefetch=2,
            grid=(B, npair),
            in_specs=[
                pl.BlockSpec((1, S, LANES), lambda b, p, ti, tj: (b, 0, p)),
                pl.BlockSpec((1, S, LANES), lambda b, p, ti, tj: (b, 0, npair + p)),
                pl.BlockSpec((1, S, LANES), lambda b, p, ti, tj: (b, 0, 2 * npair + p)),
            ],
            out_specs=pl.BlockSpec((1, S, LANES), lambda b, p, ti, tj: (b, 0, p)),
            scratch_shapes=[
                pltpu.VMEM((SB_SLOTS, 2, SB_TILE, SB_TILE), F32),
                pltpu.VMEM((SB_SLOTS, 2, SB_TILE, SB_TILE), F32),
                pltpu.VMEM((nq, 2, SB_TILE, LANES), F32),
                pltpu.VMEM((nq, 2, SB_TILE, LANES), F32),
            ],
        ),
        out_shape=jax.ShapeDtypeStruct((B, S, WIDTH), BF16),
        compiler_params=pltpu.CompilerParams(
            dimension_semantics=("parallel", "parallel"), vmem_limit_bytes=VMEM_LIMIT),
        name="sb_attn",
    )(ti, tj, qkv, qkv, qkv)


def _wkv_kernel(r_ref, w_ref, k_ref, v_ref, a_ref, lw_ref, lb_ref, rk_ref, kk_ref, ka_ref, o_ref,
                st_ref, y_scr, x_scr, o_scr, ld_scr, vec_scr):
    @pl.when(pl.program_id(1) == 0)
    def _():
        st_ref[...] = jnp.zeros_like(st_ref)

    ld_scr[...] = jnp.zeros_like(ld_scr)
    tc = r_ref.shape[0]

    def to_streams(t):
        for n, ref in enumerate((r_ref, w_ref, k_ref, v_ref, a_ref)):
            x_scr[n, t] = ref[t].T

    def to_tokens(t):
        o_ref[t] = o_scr[t].T

    def step(t, g):
        rows = slice(g * HEAD_DIM, (g + 1) * HEAD_DIM)
        a = x_scr[4, t, rows]
        k = x_scr[2, t, rows]
        kk = k * kk_ref[g]
        k = k * (1.0 + (a - 1.0) * ka_ref[g])
        norm = jnp.sqrt(jnp.sum(kk * kk, axis=0, keepdims=True))
        kkn = kk * (1.0 / jnp.maximum(norm, 1e-12))
        r = x_scr[0, t, rows]
        vt = x_scr[3, t, rows]
        g_before = jnp.exp(ld_scr[g])
        ld = ld_scr[g] + x_scr[1, t, rows]
        ld_scr[g] = ld
        g_now = jnp.exp(ld)
        g_inv = jnp.exp(-ld)
        vec_scr[g, 0] = -kkn * g_before
        vec_scr[g, 1] = kkn * a * g_inv
        vec_scr[g, 2] = k * g_inv
        vec_scr[g, 3] = r * g_now
        chan = lambda i, kc: vec_scr[g, i, kc:kc + 1, :]
        for vb0 in range(0, NVB, VB_GROUP):
            blocks = range(vb0, vb0 + VB_GROUP)
            acc = {(b, c): None for b in blocks for c in range(N_CHAINS)}
            for kc in range(HEAD_DIM):
                nrow = chan(0, kc)
                for b in blocks:
                    key = (b, kc % N_CHAINS)
                    prod = st_ref[g, b, kc] * nrow
                    acc[key] = prod if acc[key] is None else acc[key] + prod
            sa = {b: _tree_sum([acc[(b, c)] for c in range(N_CHAINS)]) for b in blocks}
            vrow = {b: vt[b * SUBLANES:(b + 1) * SUBLANES] for b in blocks}
            acc = {(b, c): None for b in blocks for c in range(N_CHAINS)}
            for kc in range(HEAD_DIM):
                brow, krow, rrow = chan(1, kc), chan(2, kc), chan(3, kc)
                for b in blocks:
                    key = (b, kc % N_CHAINS)
                    s = st_ref[g, b, kc] + sa[b] * brow + vrow[b] * krow
                    st_ref[g, b, kc] = s
                    prod = s * rrow
                    acc[key] = prod if acc[key] is None else acc[key] + prod
            for b in blocks:
                y_scr[g, b * SUBLANES:(b + 1) * SUBLANES, :] = _tree_sum(
                    [acc[(b, c)] for c in range(N_CHAINS)])
        y = y_scr[g]
        d = y - jnp.mean(y, axis=0, keepdims=True)
        var = jnp.mean(d * d, axis=0, keepdims=True)
        bonus = jnp.sum(r * k * rk_ref[g], axis=0, keepdims=True) * vt
        o_scr[t, rows] = d * lax.rsqrt(var + GN_EPS) * lw_ref[g] + lb_ref[g] + bonus

    def advance(t):
        for g in range(2):
            step(t, g)
        to_streams(jnp.minimum(t + 1, tc - 1))

    to_streams(0)
    advance(0)

    def body(t, _):
        to_tokens(t - 1)
        advance(t)
        return 0

    lax.fori_loop(1, tc, body, 0)
    to_tokens(tc - 1)
    for g in range(2):
        vec_scr[g, 0] = jnp.exp(ld_scr[g])
        for vb in range(NVB):
            for kc in range(HEAD_DIM):
                st_ref[g, vb, kc] = st_ref[g, vb, kc] * vec_scr[g, 0, kc:kc + 1, :]


def _wkv(rw, lw, lb, rk, kkc, kac, tc):
    _, S, R, _ = rw.shape
    N = HEAD_DIM
    inp = lambda n: pl.BlockSpec((None, tc, LANES, LANES), lambda g, c: (n, c, g, 0))
    const = pl.BlockSpec((2, N, LANES), lambda g, c: (0, 0, 0))
    return pl.pallas_call(
        _wkv_kernel,
        grid=(R // LANES, S // tc),
        in_specs=[inp(n) for n in range(RW_STREAMS)] + [const] * 5,
        out_specs=pl.BlockSpec((tc, LANES, LANES), lambda g, c: (c, g, 0)),
        out_shape=jax.ShapeDtypeStruct((S, R, LANES), F32),
        scratch_shapes=[pltpu.VMEM((2, NVB, N, SUBLANES, LANES), F32),
                        pltpu.VMEM((2, N, LANES), F32),
                        pltpu.VMEM((RW_STREAMS, tc, LANES, LANES), F32),
                        pltpu.VMEM((tc, LANES, LANES), F32), pltpu.VMEM((2, N, LANES), F32),
                        pltpu.VMEM((2, 4, N, LANES), F32)],
        compiler_params=pltpu.CompilerParams(
            dimension_semantics=("parallel", "arbitrary"), vmem_limit_bytes=VMEM_LIMIT),
        name="wkv",
    )(*([rw] * RW_STREAMS), lw, lb, rk, kkc, kac)


def _merge_kernel(x_ref, osb_ref, y_ref, g_ref, gpre_ref, wg_ref, bg_ref, wsb_ref, wrw_ref,
                  wo_ref, gpost_ref, x1_ref, h_scr, orw_scr, m_scr, f_scr):
    x = x_ref[...]
    h_scr[...] = _rms(x, gpre_ref[...]).astype(BF16)
    orw_scr[...] = (y_ref[...] * g_ref[...].astype(F32)).astype(BF16)
    for c0 in range(0, D_MODEL, MXU_N):
        cs = slice(c0, c0 + MXU_N)
        cr = slice(D_MODEL + c0, D_MODEL + c0 + MXU_N)
        g_sb = _sigmoid(jnp.dot(h_scr[...], wg_ref[:, cs], preferred_element_type=F32)
                        + bg_ref[:, cs])
        g_rw = _sigmoid(jnp.dot(h_scr[...], wg_ref[:, cr], preferred_element_type=F32)
                        + bg_ref[:, cr])
        a_sb = jnp.dot(osb_ref[...], wsb_ref[:, cs], preferred_element_type=F32)
        a_rw = jnp.dot(orw_scr[...], wrw_ref[:, cs], preferred_element_type=F32)
        m_scr[:, cs] = (g_sb * a_sb + g_rw * a_rw).astype(BF16)
    for c0 in range(0, D_MODEL, MXU_N):
        cs = slice(c0, c0 + MXU_N)
        f_scr[:, cs] = jnp.dot(m_scr[...], wo_ref[:, cs], preferred_element_type=F32)
    x1_ref[...] = x + _rms(f_scr[...], gpost_ref[...])


def _merge(xf, o_sb, y, g, gpre, wg, bg, wsb, wrw, wo, gpost, tm):
    T = xf.shape[0]
    spb = y.shape[0] // tm
    tile = lambda c: pl.BlockSpec((tm, c), lambda i: (i, 0))
    y_tile = pl.BlockSpec((tm, WIDTH), lambda i: (i % spb, i // spb))
    full = lambda a: pl.BlockSpec(a.shape, lambda i: (0, 0))
    return pl.pallas_call(
        _merge_kernel,
        grid=(T // tm,),
        in_specs=[tile(D_MODEL), tile(WIDTH), y_tile, tile(WIDTH), full(gpre), full(wg),
                  full(bg), full(wsb), full(wrw), full(wo), full(gpost)],
        out_specs=tile(D_MODEL),
        out_shape=jax.ShapeDtypeStruct((T, D_MODEL), F32),
        scratch_shapes=[pltpu.VMEM((tm, D_MODEL), BF16), pltpu.VMEM((tm, WIDTH), BF16),
                        pltpu.VMEM((tm, D_MODEL), BF16), pltpu.VMEM((tm, D_MODEL), F32)],
        compiler_params=pltpu.CompilerParams(
            dimension_semantics=("parallel",), vmem_limit_bytes=VMEM_LIMIT),
        name="merge",
    )(xf, o_sb, y, g, gpre, wg, bg, wsb, wrw, wo, gpost)


def _ffn_kernel(x_ref, gpre_ref, wg_ref, wu_ref, wd_ref, gpost_ref, o_ref, h_scr, acc_scr):
    x = x_ref[...]
    h_scr[...] = _rms(x, gpre_ref[...]).astype(BF16)
    for c0 in range(0, D_FF, MXU_N):
        cs = slice(c0, c0 + MXU_N)
        gt = jnp.dot(h_scr[...], wg_ref[:, cs], preferred_element_type=F32)
        up = jnp.dot(h_scr[...], wu_ref[:, cs], preferred_element_type=F32)
        act = (gt * _sigmoid(gt) * up).astype(BF16)
        part = jnp.dot(act, wd_ref[cs, :], preferred_element_type=F32)
        if c0 == 0:
            acc_scr[...] = part
        else:
            acc_scr[...] += part
    o_ref[...] = x + _rms(acc_scr[...], gpost_ref[...])


def _ffn(x1, gpre, wg, wu, wd, gpost, tm):
    T = x1.shape[0]
    tile = pl.BlockSpec((tm, D_MODEL), lambda i: (i, 0))
    full = lambda a: pl.BlockSpec(a.shape, lambda i: (0, 0))
    return pl.pallas_call(
        _ffn_kernel,
        grid=(T // tm,),
        in_specs=[tile, full(gpre), full(wg), full(wu), full(wd), full(gpost)],
        out_specs=tile,
        out_shape=jax.ShapeDtypeStruct((T, D_MODEL), F32),
        scratch_shapes=[pltpu.VMEM((tm, D_MODEL), BF16), pltpu.VMEM((tm, D_MODEL), F32)],
        compiler_params=pltpu.CompilerParams(
            dimension_semantics=("parallel",), vmem_limit_bytes=VMEM_LIMIT),
        name="ffn",
    )(x1, gpre, wg, wu, wd, gpost)


def _layer(x, norm_mix_pre, w_in, b_gate, mu_rw, w0, w_up, a0, a_up, g_up, k_k, k_a, r_k,
           lnx_w, lnx_b, w_sb_out, w_rw_out, w_o, norm_mix_post, norm_ffn_pre,
           w_ffn_gate, w_ffn_up, w_ffn_down, norm_ffn_post):
    B, S, D = x.shape
    T = B * S
    NS = B * HEADS
    assert D == D_MODEL and S % SB_TILE == 0 and (NS // 2) % LANES == 0
    tm = min(TOKEN_TILE, S)
    assert S % tm == 0
    row = lambda a: a.reshape(1, -1)

    xf = x.reshape(T, D)
    w_in_b = w_in.astype(BF16)
    wwa = jnp.zeros((W_LORA + A_LORA, 2 * WIDTH), F32)
    wwa = wwa.at[:W_LORA, :WIDTH].set(w_up).at[W_LORA:, WIDTH:].set(a_up)
    qkv, rw, g = _in_proj(xf, row(norm_mix_pre), w_in_b[:, :MIX_COLS], row(mu_rw), row(w0),
                          row(a0), wwa, g_up, S, tm)

    o_sb = _sb_attn(qkv.reshape(B, S, SB_COLS))

    lane_head = jnp.stack([2 * (jnp.arange(LANES) % (HEADS // 2)) + p for p in range(2)])
    per_stream = lambda a: a.reshape(HEADS, HEAD_DIM)[lane_head].transpose(0, 2, 1)
    y = _wkv(rw.reshape(RW_STREAMS, S, B * WIDTH // LANES, LANES), per_stream(lnx_w),
             per_stream(lnx_b), per_stream(r_k), per_stream(k_k), per_stream(k_a), min(WKV_BLOCK, S))
    y = y.reshape(S, B * WIDTH)

    x1 = _merge(xf, o_sb.reshape(T, WIDTH), y, g, row(norm_mix_pre),
                w_in_b[:, MIX_COLS:], row(b_gate), w_sb_out.astype(BF16),
                w_rw_out.astype(BF16), w_o.astype(BF16), row(norm_mix_post), tm)
    out = _ffn(x1, row(norm_ffn_pre), w_ffn_gate.astype(BF16), w_ffn_up.astype(BF16),
               w_ffn_down.astype(BF16), row(norm_ffn_post), tm)
    return out.reshape(B, S, D)


def kernel(x, norm_mix_pre, w_in, b_gate, mu_rw, w0, w_up, a0, a_up, g_up, k_k, k_a, r_k,
           lnx_w, lnx_b, w_sb_out, w_rw_out, w_o, norm_mix_post, norm_ffn_pre,
           w_ffn_gate, w_ffn_up, w_ffn_down, norm_ffn_post):
    params = (norm_mix_pre, w_in, b_gate, mu_rw, w0, w_up, a0, a_up, g_up, k_k, k_a, r_k,
              lnx_w, lnx_b, w_sb_out, w_rw_out, w_o, norm_mix_post, norm_ffn_pre,
              w_ffn_gate, w_ffn_up, w_ffn_down, norm_ffn_post)
    for l in range(norm_mix_pre.shape[0]):
        x = _layer(x, *(p[l] for p in params))
    return x
```

```python
import functools

import jax
import jax.numpy as jnp
from jax import lax
from jax.experimental import pallas as pl
from jax.experimental.pallas import tpu as pltpu

F32 = jnp.float32
BF16 = jnp.bfloat16

D_MODEL = 1024
HEADS = 8
HEAD_DIM = 64
WIDTH = HEADS * HEAD_DIM
W_LORA = 64
A_LORA = 64
G_LORA = 128
SB_COLS = 3 * WIDTH
RW_COLS = 3 * WIDTH + W_LORA + A_LORA + G_LORA
MIX_COLS = SB_COLS + RW_COLS
D_FF = 2816
RMS_EPS = 1e-6
GN_EPS = HEAD_DIM * 1e-5
LOG2E = 1.4426950408889634

LANES = 128
SUBLANES = 8
MXU_N = 256
VMEM_LIMIT = 56 * 1024 * 1024

TOKEN_TILE = 512
RW_STREAMS = 5
RW_ROWS = 256
SB_TILE = 256
SB_SLOTS = 3
SB_UNROLL = 2 * SB_SLOTS
WKV_BLOCK = 32
NVB = HEAD_DIM // SUBLANES
VB_GROUP = 4
N_CHAINS = 4


def _rms(x, g):
    ms = jnp.mean(x * x, axis=-1, keepdims=True)
    return x * lax.rsqrt(ms + RMS_EPS) * g


def _sigmoid(x):
    return 1.0 / (1.0 + jnp.exp(-x))


def _neg_abs(x):
    if x.dtype.itemsize != 4:
        return -jnp.abs(x)
    bits = lax.bitcast_convert_type(x, jnp.uint32) | jnp.uint32(0x80000000)
    return lax.bitcast_convert_type(bits, x.dtype)


def _tree_sum(xs):
    while len(xs) > 1:
        xs = [a + b for a, b in zip(xs[::2], xs[1::2])] + ([xs[-1]] if len(xs) % 2 else [])
    return xs[0]


def _softplus(x):
    return jnp.maximum(x, 0.0) + jnp.log(1.0 + jnp.exp(-jnp.abs(x)))


def _dot3(x, w_hi, w_lo):
    x_hi = x.astype(BF16)
    x_lo = (x - x_hi.astype(F32)).astype(BF16)
    return (jnp.dot(x_hi, w_hi, preferred_element_type=F32)
            + jnp.dot(x_lo, w_hi, preferred_element_type=F32)
            + jnp.dot(x_hi, w_lo, preferred_element_type=F32))


def _in_proj_kernel(tiles_per_seq, x_ref, g_ref, w_ref, mu_ref, w0_ref, a0_ref, wwa_hi_ref,
                    wwa_lo_ref, gup_hi_ref, gup_lo_ref, qkv_ref, rw_ref, gate_ref,
                    h_scr, p_scr, last_scr):
    tm = x_ref.shape[0]
    h_scr[...] = _rms(x_ref[...], g_ref[...]).astype(BF16)

    for c0 in list(range(SB_COLS, MIX_COLS, MXU_N)) + list(range(0, SB_COLS, MXU_N)):
        acc = jnp.dot(h_scr[...], w_ref[:, c0:c0 + MXU_N], preferred_element_type=F32)
        if c0 < WIDTH:
            acc = acc * (LOG2E * HEAD_DIM ** -0.5)
        if c0 < SB_COLS:
            qkv_ref[:, c0:c0 + MXU_N] = acc.astype(BF16)
        else:
            p_scr[:, c0 - SB_COLS:c0 - SB_COLS + MXU_N] = acc

    seq_start = pl.program_id(0) % tiles_per_seq == 0
    rowid = lax.broadcasted_iota(jnp.int32, (RW_ROWS, RW_COLS), 0)
    lane = lax.broadcasted_iota(jnp.int32, (RW_ROWS, W_LORA + A_LORA), 1)
    for r0 in range(0, tm, RW_ROWS):
        cur = p_scr[r0:r0 + RW_ROWS]
        if r0 == 0:
            before = jnp.where(seq_start, 0.0, last_scr[SUBLANES - 1:SUBLANES])
        else:
            before = p_scr[r0 - 1:r0]
        prev = jnp.where(rowid == 0, before, pltpu.roll(cur, 1, axis=0))
        p = cur + (prev - cur) * mu_ref[...]
        xwa = p[:, 3 * WIDTH:3 * WIDTH + W_LORA + A_LORA]
        xg = p[:, 3 * WIDTH + W_LORA + A_LORA:]
        twa = jnp.where(lane < W_LORA, jnp.tanh(xwa), xwa)
        wa = _dot3(twa, wwa_hi_ref[...], wwa_lo_ref[...])
        w_raw = w0_ref[...] + wa[:, :WIDTH]
        rs = slice(r0, r0 + RW_ROWS)
        rw_ref[0, rs] = p[:, 0:WIDTH]
        rw_ref[1, rs] = -jnp.exp(-_softplus(-w_raw) - 0.5)
        rw_ref[2, rs] = p[:, WIDTH:2 * WIDTH]
        rw_ref[3, rs] = p[:, 2 * WIDTH:3 * WIDTH]
        rw_ref[4, rs] = _sigmoid(a0_ref[...] + wa[:, WIDTH:])
        gate_ref[rs] = _dot3(_sigmoid(xg), gup_hi_ref[...], gup_lo_ref[...]).astype(BF16)
    last_scr[...] = p_scr[tm - SUBLANES:tm]


def _in_proj(xf, g, w, mu, w0, a0, wwa, g_up, seq_len, tm):
    T = xf.shape[0]
    spb = seq_len // tm
    hi = lambda a: a.astype(BF16)
    lo = lambda a: (a - a.astype(BF16).astype(F32)).astype(BF16)
    full = lambda a: pl.BlockSpec(a.shape, lambda i: (0, 0))
    consts = (g, w, mu, w0, a0, hi(wwa), lo(wwa), hi(g_up), lo(g_up))
    return pl.pallas_call(
        functools.partial(_in_proj_kernel, spb),
        grid=(T // tm,),
        in_specs=[pl.BlockSpec((tm, D_MODEL), lambda i: (i, 0))] + [full(a) for a in consts],
        out_specs=[
            pl.BlockSpec((tm, SB_COLS), lambda i: (i, 0)),
            pl.BlockSpec((RW_STREAMS, tm, WIDTH), lambda i: (0, i % spb, i // spb)),
            pl.BlockSpec((tm, WIDTH), lambda i: (i, 0)),
        ],
        out_shape=[
            jax.ShapeDtypeStruct((T, SB_COLS), BF16),
            jax.ShapeDtypeStruct((RW_STREAMS, seq_len, (T // seq_len) * WIDTH), F32),
            jax.ShapeDtypeStruct((T, WIDTH), BF16),
        ],
        scratch_shapes=[pltpu.VMEM((tm, D_MODEL), BF16), pltpu.VMEM((tm, RW_COLS), F32),
                        pltpu.VMEM((SUBLANES, RW_COLS), F32)],
        compiler_params=pltpu.CompilerParams(
            dimension_semantics=("arbitrary",), vmem_limit_bytes=VMEM_LIMIT),
        name="in_proj",
    )(xf, *consts)


def _sb_kernel(ti_ref, tj_ref, q_ref, k_ref, v_ref, o_ref, z_scr, c_scr, acc_scr, car_scr):
    S = q_ref.shape[1]
    nq = S // SB_TILE
    n_off = nq * (nq - 1) // 2
    row = lax.broadcasted_iota(jnp.int32, (SB_TILE, SB_TILE), 0)
    col = lax.broadcasted_iota(jnp.int32, (SB_TILE, SB_TILE), 1)
    incl_mat = jnp.where(row >= col, 1.0, 0.0).astype(BF16)
    strict = col < row
    head0 = lax.broadcasted_iota(jnp.int32, (SB_TILE, LANES), 1) < HEAD_DIM

    def rows(ref, t):
        return ref[0, pl.ds(pl.multiple_of(t * SB_TILE, SB_TILE), SB_TILE), :]

    def issue_logits(i, j, slot):
        q = rows(q_ref, i)
        kb = rows(k_ref, j)
        for h in range(2):
            qh = jnp.where(head0 if h == 0 else ~head0, q, jnp.zeros_like(q))
            z_scr[slot, h] = lax.dot_general(qh, kb, (((1,), (1,)), ((), ())),
                                             preferred_element_type=F32)

    def cumulate(slot, diag):
        for h in range(2):
            z = z_scr[slot, h]
            s = jnp.maximum(z, 0.0) + jnp.log(1.0 + jnp.exp2(_neg_abs(z))) * LOG2E
            if diag:
                s = jnp.where(strict, s, 0.0)
            c_scr[slot, h] = jnp.dot(s.astype(BF16), incl_mat, preferred_element_type=F32)

    def finish(i, j, slot, diag):
        vb = rows(v_ref, j)
        for h in range(2):
            c = c_scr[slot, h]
            e = z_scr[slot, h] - c
            if not diag:
                car = car_scr[i, h]
                e = e - jnp.concatenate([car, car], axis=1)
            w = jnp.exp2(e)
            if diag:
                w = jnp.where(strict, w, 0.0)
            pv = jnp.dot(w.astype(BF16), vb, preferred_element_type=F32)
            tot = jnp.broadcast_to(c[:, 0:1], (SB_TILE, LANES))
            if diag:
                acc_scr[i, h] = pv
                car_scr[i, h] = tot
            else:
                acc_scr[i, h] += pv
                car_scr[i, h] += tot

    def run(steps, tile_of, diag):
        def step(n, slot):
            issue_logits(*tile_of(jnp.minimum(n + 1, steps - 1)), (slot + 1) % SB_SLOTS)
            finish(*tile_of(n - 1), (slot - 1) % SB_SLOTS, diag)
            cumulate(slot, diag)

        issue_logits(*tile_of(0), 0)
        issue_logits(*tile_of(min(1, steps - 1)), 1)
        cumulate(0, diag)
        groups = (steps - 1) // SB_UNROLL

        def body(g, _):
            for u in range(SB_UNROLL):
                step(1 + SB_UNROLL * g + u, (1 + u) % SB_SLOTS)
            return 0

        lax.fori_loop(0, groups, body, 0)
        for n in range(1 + SB_UNROLL * groups, steps):
            step(n, n % SB_SLOTS)
        finish(*tile_of(steps - 1), (steps - 1) % SB_SLOTS, diag)

    run(nq, lambda n: (n, n), True)
    if n_off:
        run(n_off, lambda n: (ti_ref[n], tj_ref[n]), False)
    for i in range(nq):
        o_ref[0, i * SB_TILE:(i + 1) * SB_TILE, :] = jnp.where(
            head0, acc_scr[i, 0], acc_scr[i, 1]).astype(o_ref.dtype)


def _sb_attn(qkv):
    B, S, _ = qkv.shape
    npair = WIDTH // LANES
    nq = S // SB_TILE
    pairs = [(i, j) for i in range(nq) for j in range(i - 1, -1, -1)] or [(0, 0)]
    ti = jnp.array([p[0] for p in pairs], jnp.int32)
    tj = jnp.array([p[1] for p in pairs], jnp.int32)
    return pl.pallas_call(
        _sb_kernel,
        grid_spec=pltpu.PrefetchScalarGridSpec(
            num_scalar_prefetch=2,
            grid=(B, npair),
            in_specs=[
                pl.BlockSpec((1, S, LANES), lambda b, p, ti, tj: (b, 0, p)),
                pl.BlockSpec((1, S, LANES), lambda b, p, ti, tj: (b, 0, npair + p)),
                pl.BlockSpec((1, S, LANES), lambda b, p, ti, tj: (b, 0, 2 * npair + p)),
            ],
            out_specs=pl.BlockSpec((1, S, LANES), lambda b, p, ti, tj: (b, 0, p)),
            scratch_shapes=[
                pltpu.VMEM((SB_SLOTS, 2, SB_TILE, SB_TILE), F32),
                pltpu.VMEM((SB_SLOTS, 2, SB_TILE, SB_TILE), F32),
                pltpu.VMEM((nq, 2, SB_TILE, LANES), F32),
                pltpu.VMEM((nq, 2, SB_TILE, LANES), F32),
            ],
        ),
        out_shape=jax.ShapeDtypeStruct((B, S, WIDTH), BF16),
        compiler_params=pltpu.CompilerParams(
            dimension_semantics=("parallel", "parallel"), vmem_limit_bytes=VMEM_LIMIT),
        name="sb_attn",
    )(ti, tj, qkv, qkv, qkv)


def _wkv_kernel(r_ref, w_ref, k_ref, v_ref, a_ref, lw_ref, lb_ref, rk_ref, kk_ref, ka_ref, o_ref,
                st_ref, y_scr, x_scr, o_scr, ld_scr, vec_scr):
    @pl.when(pl.program_id(1) == 0)
    def _():
        st_ref[...] = jnp.zeros_like(st_ref)

    ld_scr[...] = jnp.zeros_like(ld_scr)
    tc = r_ref.shape[0]

    def to_streams(t):
        for n, ref in enumerate((r_ref, w_ref, k_ref, v_ref, a_ref)):
            x_scr[n, t] = ref[t].T

    def to_tokens(t):
        o_ref[t] = o_scr[t].T

    def step(t, g):
        rows = slice(g * HEAD_DIM, (g + 1) * HEAD_DIM)
        a = x_scr[4, t, rows]
        k = x_scr[2, t, rows]
        kk = k * kk_ref[g]
        k = k * (1.0 + (a - 1.0) * ka_ref[g])
        norm = jnp.sqrt(jnp.sum(kk * kk, axis=0, keepdims=True))
        kkn = kk * (1.0 / jnp.maximum(norm, 1e-12))
        r = x_scr[0, t, rows]
        vt = x_scr[3, t, rows]
        g_before = jnp.exp(ld_scr[g])
        ld = ld_scr[g] + x_scr[1, t, rows]
        ld_scr[g] = ld
        g_now = jnp.exp(ld)
        g_inv = jnp.exp(-ld)
        vec_scr[g, 0] = -kkn * g_before
        vec_scr[g, 1] = kkn * a * g_inv
        vec_scr[g, 2] = k * g_inv
        vec_scr[g, 3] = r * g_now
        chan = lambda i, kc: vec_scr[g, i, kc:kc + 1, :]
        for vb0 in range(0, NVB, VB_GROUP):
            blocks = range(vb0, vb0 + VB_GROUP)
            acc = {(b, c): None for b in blocks for c in range(N_CHAINS)}
            for kc in range(HEAD_DIM):
                nrow = chan(0, kc)
                for b in blocks:
                    key = (b, kc % N_CHAINS)
                    prod = st_ref[g, b, kc] * nrow
                    acc[key] = prod if acc[key] is None else acc[key] + prod
            sa = {b: _tree_sum([acc[(b, c)] for c in range(N_CHAINS)]) for b in blocks}
            vrow = {b: vt[b * SUBLANES:(b + 1) * SUBLANES] for b in blocks}
            acc = {(b, c): None for b in blocks for c in range(N_CHAINS)}
            for kc in range(HEAD_DIM):
                brow, krow, rrow = chan(1, kc), chan(2, kc), chan(3, kc)
                for b in blocks:
                    key = (b, kc % N_CHAINS)
                    s = st_ref[g, b, kc] + sa[b] * brow + vrow[b] * krow
                    st_ref[g, b, kc] = s
                    prod = s * rrow
                    acc[key] = prod if acc[key] is None else acc[key] + prod
            for b in blocks:
                y_scr[g, b * SUBLANES:(b + 1) * SUBLANES, :] = _tree_sum(
                    [acc[(b, c)] for c in range(N_CHAINS)])
        y = y_scr[g]
        d = y - jnp.mean(y, axis=0, keepdims=True)
        var = jnp.mean(d * d, axis=0, keepdims=True)
        bonus = jnp.sum(r * k * rk_ref[g], axis=0, keepdims=True) * vt
        o_scr[t, rows] = d * lax.rsqrt(var + GN_EPS) * lw_ref[g] + lb_ref[g] + bonus

    def advance(t):
        for g in range(2):
            step(t, g)
        to_streams(jnp.minimum(t + 1, tc - 1))

    to_streams(0)
    advance(0)

    def body(t, _):
        to_tokens(t - 1)
        advance(t)
        return 0

    lax.fori_loop(1, tc, body, 0)
    to_tokens(tc - 1)
    for g in range(2):
        vec_scr[g, 0] = jnp.exp(ld_scr[g])
        for vb in range(NVB):
            for kc in range(HEAD_DIM):
                st_ref[g, vb, kc] = st_ref[g, vb, kc] * vec_scr[g, 0, kc:kc + 1, :]


def _wkv(rw, lw, lb, rk, kkc, kac, tc):
    _, S, R, _ = rw.shape
    N = HEAD_DIM
    inp = lambda n: pl.BlockSpec((None, tc, LANES, LANES), lambda g, c: (n, c, g, 0))
    const = pl.BlockSpec((2, N, LANES), lambda g, c: (0, 0, 0))
    return pl.pallas_call(
        _wkv_kernel,
        grid=(R // LANES, S // tc),
        in_specs=[inp(n) for n in range(RW_STREAMS)] + [const] * 5,
        out_specs=pl.BlockSpec((tc, LANES, LANES), lambda g, c: (c, g, 0)),
        out_shape=jax.ShapeDtypeStruct((S, R, LANES), F32),
        scratch_shapes=[pltpu.VMEM((2, NVB, N, SUBLANES, LANES), F32),
                        pltpu.VMEM((2, N, LANES), F32),
                        pltpu.VMEM((RW_STREAMS, tc, LANES, LANES), F32),
                        pltpu.VMEM((tc, LANES, LANES), F32), pltpu.VMEM((2, N, LANES), F32),
                        pltpu.VMEM((2, 4, N, LANES), F32)],
        compiler_params=pltpu.CompilerParams(
            dimension_semantics=("parallel", "arbitrary"), vmem_limit_bytes=VMEM_LIMIT),
        name="wkv",
    )(*([rw] * RW_STREAMS), lw, lb, rk, kkc, kac)


def _merge_kernel(x_ref, osb_ref, y_ref, g_ref, gpre_ref, wg_ref, bg_ref, wsb_ref, wrw_ref,
                  wo_ref, gpost_ref, x1_ref, h_scr, orw_scr, m_scr, f_scr):
    x = x_ref[...]
    h_scr[...] = _rms(x, gpre_ref[...]).astype(BF16)
    orw_scr[...] = (y_ref[...] * g_ref[...].astype(F32)).astype(BF16)
    for c0 in range(0, D_MODEL, MXU_N):
        cs = slice(c0, c0 + MXU_N)
        cr = slice(D_MODEL + c0, D_MODEL + c0 + MXU_N)
        g_sb = _sigmoid(jnp.dot(h_scr[...], wg_ref[:, cs], preferred_element_type=F32)
                        + bg_ref[:, cs])
        g_rw = _sigmoid(jnp.dot(h_scr[...], wg_ref[:, cr], preferred_element_type=F32)
                        + bg_ref[:, cr])
        a_sb = jnp.dot(osb_ref[...], wsb_ref[:, cs], preferred_element_type=F32)
        a_rw = jnp.dot(orw_scr[...], wrw_ref[:, cs], preferred_element_type=F32)
        m_scr[:, cs] = (g_sb * a_sb + g_rw * a_rw).astype(BF16)
    for c0 in range(0, D_MODEL, MXU_N):
        cs = slice(c0, c0 + MXU_N)
        f_scr[:, cs] = jnp.dot(m_scr[...], wo_ref[:, cs], preferred_element_type=F32)
    x1_ref[...] = x + _rms(f_scr[...], gpost_ref[...])


def _merge(xf, o_sb, y, g, gpre, wg, bg, wsb, wrw, wo, gpost, tm):
    T = xf.shape[0]
    spb = y.shape[0] // tm
    tile = lambda c: pl.BlockSpec((tm, c), lambda i: (i, 0))
    y_tile = pl.BlockSpec((tm, WIDTH), lambda i: (i % spb, i // spb))
    full = lambda a: pl.BlockSpec(a.shape, lambda i: (0, 0))
    return pl.pallas_call(
        _merge_kernel,
        grid=(T // tm,),
        in_specs=[tile(D_MODEL), tile(WIDTH), y_tile, tile(WIDTH), full(gpre), full(wg),
                  full(bg), full(wsb), full(wrw), full(wo), full(gpost)],
        out_specs=tile(D_MODEL),
        out_shape=jax.ShapeDtypeStruct((T, D_MODEL), F32),
        scratch_shapes=[pltpu.VMEM((tm, D_MODEL), BF16), pltpu.VMEM((tm, WIDTH), BF16),
                        pltpu.VMEM((tm, D_MODEL), BF16), pltpu.VMEM((tm, D_MODEL), F32)],
        compiler_params=pltpu.CompilerParams(
            dimension_semantics=("parallel",), vmem_limit_bytes=VMEM_LIMIT),
        name="merge",
    )(xf, o_sb, y, g, gpre, wg, bg, wsb, wrw, wo, gpost)


def _ffn_kernel(x_ref, gpre_ref, wg_ref, wu_ref, wd_ref, gpost_ref, o_ref, h_scr, acc_scr):
    x = x_ref[...]
    h_scr[...] = _rms(x, gpre_ref[...]).astype(BF16)
    for c0 in range(0, D_FF, MXU_N):
        cs = slice(c0, c0 + MXU_N)
        gt = jnp.dot(h_scr[...], wg_ref[:, cs], preferred_element_type=F32)
        up = jnp.dot(h_scr[...], wu_ref[:, cs], preferred_element_type=F32)
        act = (gt * _sigmoid(gt) * up).astype(BF16)
        part = jnp.dot(act, wd_ref[cs, :], preferred_element_type=F32)
        if c0 == 0:
            acc_scr[...] = part
        else:
            acc_scr[...] += part
    o_ref[...] = x + _rms(acc_scr[...], gpost_ref[...])


def _ffn(x1, gpre, wg, wu, wd, gpost, tm):
    T = x1.shape[0]
    tile = pl.BlockSpec((tm, D_MODEL), lambda i: (i, 0))
    full = lambda a: pl.BlockSpec(a.shape, lambda i: (0, 0))
    return pl.pallas_call(
        _ffn_kernel,
        grid=(T // tm,),
        in_specs=[tile, full(gpre), full(wg), full(wu), full(wd), full(gpost)],
        out_specs=tile,
        out_shape=jax.ShapeDtypeStruct((T, D_MODEL), F32),
        scratch_shapes=[pltpu.VMEM((tm, D_MODEL), BF16), pltpu.VMEM((tm, D_MODEL), F32)],
        compiler_params=pltpu.CompilerParams(
            dimension_semantics=("parallel",), vmem_limit_bytes=VMEM_LIMIT),
        name="ffn",
    )(x1, gpre, wg, wu, wd, gpost)


def _layer(x, norm_mix_pre, w_in, b_gate, mu_rw, w0, w_up, a0, a_up, g_up, k_k, k_a, r_k,
           lnx_w, lnx_b, w_sb_out, w_rw_out, w_o, norm_mix_post, norm_ffn_pre,
           w_ffn_gate, w_ffn_up, w_ffn_down, norm_ffn_post):
    B, S, D = x.shape
    T = B * S
    NS = B * HEADS
    assert D == D_MODEL and S % SB_TILE == 0 and (NS // 2) % LANES == 0
    tm = min(TOKEN_TILE, S)
    assert S % tm == 0
    row = lambda a: a.reshape(1, -1)

    xf = x.reshape(T, D)
    w_in_b = w_in.astype(BF16)
    wwa = jnp.zeros((W_LORA + A_LORA, 2 * WIDTH), F32)
    wwa = wwa.at[:W_LORA, :WIDTH].set(w_up).at[W_LORA:, WIDTH:].set(a_up)
    qkv, rw, g = _in_proj(xf, row(norm_mix_pre), w_in_b[:, :MIX_COLS], row(mu_rw), row(w0),
                          row(a0), wwa, g_up, S, tm)

    o_sb = _sb_attn(qkv.reshape(B, S, SB_COLS))

    lane_head = jnp.stack([2 * (jnp.arange(LANES) % (HEADS // 2)) + p for p in range(2)])
    per_stream = lambda a: a.reshape(HEADS, HEAD_DIM)[lane_head].transpose(0, 2, 1)
    y = _wkv(rw.reshape(RW_STREAMS, S, B * WIDTH // LANES, LANES), per_stream(lnx_w),
             per_stream(lnx_b), per_stream(r_k), per_stream(k_k), per_stream(k_a), min(WKV_BLOCK, S))
    y = y.reshape(S, B * WIDTH)

    x1 = _merge(xf, o_sb.reshape(T, WIDTH), y, g, row(norm_mix_pre),
                w_in_b[:, MIX_COLS:], row(b_gate), w_sb_out.astype(BF16),
                w_rw_out.astype(BF16), w_o.astype(BF16), row(norm_mix_post), tm)
    out = _ffn(x1, row(norm_ffn_pre), w_ffn_gate.astype(BF16), w_ffn_up.astype(BF16),
               w_ffn_down.astype(BF16), row(norm_ffn_post), tm)
    return out.reshape(B, S, D)


def kernel(x, norm_mix_pre, w_in, b_gate, mu_rw, w0, w_up, a0, a_up, g_up, k_k, k_a, r_k,
           lnx_w, lnx_b, w_sb_out, w_rw_out, w_o, norm_mix_post, norm_ffn_pre,
           w_ffn_gate, w_ffn_up, w_ffn_down, norm_ffn_post):
    params = (norm_mix_pre, w_in, b_gate, mu_rw, w0, w_up, a0, a_up, g_up, k_k, k_a, r_k,
              lnx_w, lnx_b, w_sb_out, w_rw_out, w_o, norm_mix_post, norm_ffn_pre,
              w_ffn_gate, w_ffn_up, w_ffn_down, norm_ffn_post)
    for l in range(norm_mix_pre.shape[0]):
        x = _layer(x, *(p[l] for p in params))
    return x
```

```python
import functools

import jax
import jax.numpy as jnp
from jax import lax
from jax.experimental import pallas as pl
from jax.experimental.pallas import tpu as pltpu

F32 = jnp.float32
BF16 = jnp.bfloat16

D_MODEL = 1024
HEADS = 8
HEAD_DIM = 64
WIDTH = HEADS * HEAD_DIM
W_LORA = 64
A_LORA = 64
G_LORA = 128
SB_COLS = 3 * WIDTH
RW_COLS = 3 * WIDTH + W_LORA + A_LORA + G_LORA
MIX_COLS = SB_COLS + RW_COLS
D_FF = 2816
RMS_EPS = 1e-6
GN_EPS = HEAD_DIM * 1e-5
LOG2E = 1.4426950408889634

LANES = 128
SUBLANES = 8
MXU_N = 256
VMEM_LIMIT = 56 * 1024 * 1024

TOKEN_TILE = 512
POST_TILE = 1024
RW_STREAMS = 5
RW_ROWS = 256
SB_TILE = 256
SB_SLOTS = 3
SB_UNROLL = 2 * SB_SLOTS
WKV_BLOCK = 32
NVB = HEAD_DIM // SUBLANES
VB_GROUP = 4
N_CHAINS = 4


def _rms(x, g):
    ms = jnp.mean(x * x, axis=-1, keepdims=True)
    return x * lax.rsqrt(ms + RMS_EPS) * g


def _sigmoid(x):
    return 1.0 / (1.0 + jnp.exp(-x))


def _resident(a):
    return pl.BlockSpec(a.shape, lambda i: (0, 0), pipeline_mode=pl.Buffered(1))


def _neg_abs(x):
    if x.dtype.itemsize != 4:
        return -jnp.abs(x)
    bits = lax.bitcast_convert_type(x, jnp.uint32) | jnp.uint32(0x80000000)
    return lax.bitcast_convert_type(bits, x.dtype)


def _tree_sum(xs):
    while len(xs) > 1:
        xs = [a + b for a, b in zip(xs[::2], xs[1::2])] + ([xs[-1]] if len(xs) % 2 else [])
    return xs[0]


def _softplus(x):
    return jnp.maximum(x, 0.0) + jnp.log(1.0 + jnp.exp(-jnp.abs(x)))


def _dot3(x, w_hi, w_lo):
    x_hi = x.astype(BF16)
    x_lo = (x - x_hi.astype(F32)).astype(BF16)
    return (jnp.dot(x_hi, w_hi, preferred_element_type=F32)
            + jnp.dot(x_lo, w_hi, preferred_element_type=F32)
            + jnp.dot(x_hi, w_lo, preferred_element_type=F32))


def _in_proj_kernel(tiles_per_seq, x_ref, g_ref, w_ref, mu_ref, w0_ref, a0_ref, wwa_hi_ref,
                    wwa_lo_ref, gup_hi_ref, gup_lo_ref, qkv_ref, rw_ref, gate_ref,
                    h_scr, p_scr, last_scr):
    tm = x_ref.shape[0]
    h_scr[...] = _rms(x_ref[...], g_ref[...]).astype(BF16)

    for c0 in list(range(SB_COLS, MIX_COLS, MXU_N)) + list(range(0, SB_COLS, MXU_N)):
        acc = jnp.dot(h_scr[...], w_ref[:, c0:c0 + MXU_N], preferred_element_type=F32)
        if c0 < WIDTH:
            acc = acc * (LOG2E * HEAD_DIM ** -0.5)
        if c0 < SB_COLS:
            qkv_ref[:, c0:c0 + MXU_N] = acc.astype(BF16)
        else:
            p_scr[:, c0 - SB_COLS:c0 - SB_COLS + MXU_N] = acc

    seq_start = pl.program_id(0) % tiles_per_seq == 0
    rowid = lax.broadcasted_iota(jnp.int32, (RW_ROWS, RW_COLS), 0)
    lane = lax.broadcasted_iota(jnp.int32, (RW_ROWS, W_LORA + A_LORA), 1)
    for r0 in range(0, tm, RW_ROWS):
        cur = p_scr[r0:r0 + RW_ROWS]
        if r0 == 0:
            before = jnp.where(seq_start, 0.0, last_scr[SUBLANES - 1:SUBLANES])
        else:
            before = p_scr[r0 - 1:r0]
        prev = jnp.where(rowid == 0, before, pltpu.roll(cur, 1, axis=0))
        p = cur + (prev - cur) * mu_ref[...]
        xwa = p[:, 3 * WIDTH:3 * WIDTH + W_LORA + A_LORA]
        xg = p[:, 3 * WIDTH + W_LORA + A_LORA:]
        twa = jnp.where(lane < W_LORA, jnp.tanh(xwa), xwa)
        wa = _dot3(twa, wwa_hi_ref[...], wwa_lo_ref[...])
        w_raw = w0_ref[...] + wa[:, :WIDTH]
        rs = slice(r0, r0 + RW_ROWS)
        rw_ref[0, rs] = p[:, 0:WIDTH]
        rw_ref[1, rs] = -jnp.exp(-_softplus(-w_raw) - 0.5)
        rw_ref[2, rs] = p[:, WIDTH:2 * WIDTH]
        rw_ref[3, rs] = p[:, 2 * WIDTH:3 * WIDTH]
        rw_ref[4, rs] = _sigmoid(a0_ref[...] + wa[:, WIDTH:])
        gate_ref[rs] = _dot3(_sigmoid(xg), gup_hi_ref[...], gup_lo_ref[...]).astype(BF16)
    last_scr[...] = p_scr[tm - SUBLANES:tm]


def _in_proj(xf, g, w, mu, w0, a0, wwa, g_up, seq_len, tm):
    T = xf.shape[0]
    spb = seq_len // tm
    hi = lambda a: a.astype(BF16)
    lo = lambda a: (a - a.astype(BF16).astype(F32)).astype(BF16)
    full = lambda a: pl.BlockSpec(a.shape, lambda i: (0, 0))
    consts = (g, w, mu, w0, a0, hi(wwa), lo(wwa), hi(g_up), lo(g_up))
    return pl.pallas_call(
        functools.partial(_in_proj_kernel, spb),
        grid=(T // tm,),
        in_specs=[pl.BlockSpec((tm, D_MODEL), lambda i: (i, 0))] + [full(a) for a in consts],
        out_specs=[
            pl.BlockSpec((tm, SB_COLS), lambda i: (i, 0)),
            pl.BlockSpec((RW_STREAMS, tm, WIDTH), lambda i: (0, i % spb, i // spb)),
            pl.BlockSpec((tm, WIDTH), lambda i: (i, 0)),
        ],
        out_shape=[
            jax.ShapeDtypeStruct((T, SB_COLS), BF16),
            jax.ShapeDtypeStruct((RW_STREAMS, seq_len, (T // seq_len) * WIDTH), F32),
            jax.ShapeDtypeStruct((T, WIDTH), BF16),
        ],
        scratch_shapes=[pltpu.VMEM((tm, D_MODEL), BF16), pltpu.VMEM((tm, RW_COLS), F32),
                        pltpu.VMEM((SUBLANES, RW_COLS), F32)],
        compiler_params=pltpu.CompilerParams(
            dimension_semantics=("arbitrary",), vmem_limit_bytes=VMEM_LIMIT),
        name="in_proj",
    )(xf, *consts)


def _sb_kernel(ti_ref, tj_ref, q_ref, k_ref, v_ref, o_ref, z_scr, c_scr, acc_scr, car_scr):
    S = q_ref.shape[1]
    nq = S // SB_TILE
    n_off = nq * (nq - 1) // 2
    row = lax.broadcasted_iota(jnp.int32, (SB_TILE, SB_TILE), 0)
    col = lax.broadcasted_iota(jnp.int32, (SB_TILE, SB_TILE), 1)
    incl_mat = jnp.where(row >= col, 1.0, 0.0).astype(BF16)
    strict = col < row
    head0 = lax.broadcasted_iota(jnp.int32, (SB_TILE, LANES), 1) < HEAD_DIM

    def rows(ref, t):
        return ref[0, pl.ds(pl.multiple_of(t * SB_TILE, SB_TILE), SB_TILE), :]

    def issue_logits(i, j, slot):
        q = rows(q_ref, i)
        kb = rows(k_ref, j)
        for h in range(2):
            qh = jnp.where(head0 if h == 0 else ~head0, q, jnp.zeros_like(q))
            z_scr[slot, h] = lax.dot_general(qh, kb, (((1,), (1,)), ((), ())),
                                             preferred_element_type=F32)

    def cumulate(slot, diag):
        for h in range(2):
            z = z_scr[slot, h]
            s = jnp.maximum(z, 0.0) + jnp.log(1.0 + jnp.exp2(_neg_abs(z))) * LOG2E
            if diag:
                s = jnp.where(strict, s, 0.0)
            c_scr[slot, h] = jnp.dot(s.astype(BF16), incl_mat, preferred_element_type=F32)

    def finish(i, j, slot, diag):
        vb = rows(v_ref, j)
        for h in range(2):
            c = c_scr[slot, h]
            e = z_scr[slot, h] - c
            if not diag:
                car = car_scr[i, h]
                e = e - jnp.concatenate([car, car], axis=1)
            w = jnp.exp2(e)
            if diag:
                w = jnp.where(strict, w, 0.0)
            pv = jnp.dot(w.astype(BF16), vb, preferred_element_type=F32)
            tot = jnp.broadcast_to(c[:, 0:1], (SB_TILE, LANES))
            if diag:
                acc_scr[i, h] = pv
                car_scr[i, h] = tot
            else:
                acc_scr[i, h] += pv
                car_scr[i, h] += tot

    def run(steps, tile_of, diag):
        def step(n, slot):
            issue_logits(*tile_of(jnp.minimum(n + 1, steps - 1)), (slot + 1) % SB_SLOTS)
            finish(*tile_of(n - 1), (slot - 1) % SB_SLOTS, diag)
            cumulate(slot, diag)

        issue_logits(*tile_of(0), 0)
        issue_logits(*tile_of(min(1, steps - 1)), 1)
        cumulate(0, diag)
        groups = (steps - 1) // SB_UNROLL

        def body(g, _):
            for u in range(SB_UNROLL):
                step(1 + SB_UNROLL * g + u, (1 + u) % SB_SLOTS)
            return 0

        lax.fori_loop(0, groups, body, 0)
        for n in range(1 + SB_UNROLL * groups, steps):
            step(n, n % SB_SLOTS)
        finish(*tile_of(steps - 1), (steps - 1) % SB_SLOTS, diag)

    run(nq, lambda n: (n, n), True)
    if n_off:
        run(n_off, lambda n: (ti_ref[n], tj_ref[n]), False)
    for i in range(nq):
        o_ref[0, i * SB_TILE:(i + 1) * SB_TILE, :] = jnp.where(
            head0, acc_scr[i, 0], acc_scr[i, 1]).astype(o_ref.dtype)


def _sb_attn(qkv):
    B, S, _ = qkv.shape
    npair = WIDTH // LANES
    nq = S // SB_TILE
    pairs = [(i, j) for i in range(nq) for j in range(i - 1, -1, -1)] or [(0, 0)]
    ti = jnp.array([p[0] for p in pairs], jnp.int32)
    tj = jnp.array([p[1] for p in pairs], jnp.int32)
    return pl.pallas_call(
        _sb_kernel,
        grid_spec=pltpu.PrefetchScalarGridSpec(
            num_scalar_prefetch=2,
            grid=(B, npair),
            in_specs=[
                pl.BlockSpec((1, S, LANES), lambda b, p, ti, tj: (b, 0, p)),
                pl.BlockSpec((1, S, LANES), lambda b, p, ti, tj: (b, 0, npair + p)),
                pl.BlockSpec((1, S, LANES), lambda b, p, ti, tj: (b, 0, 2 * npair + p)),
            ],
            out_specs=pl.BlockSpec((1, S, LANES), lambda b, p, ti, tj: (b, 0, p)),
            scratch_shapes=[
                pltpu.VMEM((SB_SLOTS, 2, SB_TILE, SB_TILE), F32),
                pltpu.VMEM((SB_SLOTS, 2, SB_TILE, SB_TILE), F32),
                pltpu.VMEM((nq, 2, SB_TILE, LANES), F32),
                pltpu.VMEM((nq, 2, SB_TILE, LANES), F32),
            ],
        ),
        out_shape=jax.ShapeDtypeStruct((B, S, WIDTH), BF16),
        compiler_params=pltpu.CompilerParams(
            dimension_semantics=("parallel", "parallel"), vmem_limit_bytes=VMEM_LIMIT),
        name="sb_attn",
    )(ti, tj, qkv, qkv, qkv)


def _wkv_kernel(r_ref, w_ref, k_ref, v_ref, a_ref, lw_ref, lb_ref, rk_ref, kk_ref, ka_ref, o_ref,
                st_ref, y_scr, x_scr, o_scr, ld_scr, vec_scr):
    @pl.when(pl.program_id(1) == 0)
    def _():
        st_ref[...] = jnp.zeros_like(st_ref)

    ld_scr[...] = jnp.zeros_like(ld_scr)
    tc = r_ref.shape[0]

    def to_streams(t):
        for n, ref in enumerate((r_ref, w_ref, k_ref, v_ref, a_ref)):
            x_scr[n, t] = ref[t].T

    def to_tokens(t):
        o_ref[t] = o_scr[t].T

    def step(t, g):
        rows = slice(g * HEAD_DIM, (g + 1) * HEAD_DIM)
        a = x_scr[4, t, rows]
        k = x_scr[2, t, rows]
        kk = k * kk_ref[g]
        k = k * (1.0 + (a - 1.0) * ka_ref[g])
        norm = jnp.sqrt(jnp.sum(kk * kk, axis=0, keepdims=True))
        kkn = kk * (1.0 / jnp.maximum(norm, 1e-12))
        r = x_scr[0, t, rows]
        vt = x_scr[3, t, rows]
        g_before = jnp.exp(ld_scr[g])
        ld = ld_scr[g] + x_scr[1, t, rows]
        ld_scr[g] = ld
        g_now = jnp.exp(ld)
        g_inv = jnp.exp(-ld)
        vec_scr[g, 0] = -kkn * g_before
        vec_scr[g, 1] = kkn * a * g_inv
        vec_scr[g, 2] = k * g_inv
        vec_scr[g, 3] = r * g_now
        chan = lambda i, kc: vec_scr[g, i, kc:kc + 1, :]
        for vb0 in range(0, NVB, VB_GROUP):
            blocks = range(vb0, vb0 + VB_GROUP)
            acc = {(b, c): None for b in blocks for c in range(N_CHAINS)}
            for kc in range(HEAD_DIM):
                nrow = chan(0, kc)
                for b in blocks:
                    key = (b, kc % N_CHAINS)
                    prod = st_ref[g, b, kc] * nrow
                    acc[key] = prod if acc[key] is None else acc[key] + prod
            sa = {b: _tree_sum([acc[(b, c)] for c in range(N_CHAINS)]) for b in blocks}
            vrow = {b: vt[b * SUBLANES:(b + 1) * SUBLANES] for b in blocks}
            acc = {(b, c): None for b in blocks for c in range(N_CHAINS)}
            for kc in range(HEAD_DIM):
                brow, krow, rrow = chan(1, kc), chan(2, kc), chan(3, kc)
                for b in blocks:
                    key = (b, kc % N_CHAINS)
                    s = st_ref[g, b, kc] + sa[b] * brow + vrow[b] * krow
                    st_ref[g, b, kc] = s
                    prod = s * rrow
                    acc[key] = prod if acc[key] is None else acc[key] + prod
            for b in blocks:
                y_scr[g, b * SUBLANES:(b + 1) * SUBLANES, :] = _tree_sum(
                    [acc[(b, c)] for c in range(N_CHAINS)])
        y = y_scr[g]
        d = y - jnp.mean(y, axis=0, keepdims=True)
        var = jnp.mean(d * d, axis=0, keepdims=True)
        bonus = jnp.sum(r * k * rk_ref[g], axis=0, keepdims=True) * vt
        o_scr[t, rows] = d * lax.rsqrt(var + GN_EPS) * lw_ref[g] + lb_ref[g] + bonus

    def advance(t):
        for g in range(2):
            step(t, g)
        to_streams(jnp.minimum(t + 1, tc - 1))

    to_streams(0)
    advance(0)

    def body(t, _):
        to_tokens(t - 1)
        advance(t)
        return 0

    lax.fori_loop(1, tc, body, 0)
    to_tokens(tc - 1)
    for g in range(2):
        vec_scr[g, 0] = jnp.exp(ld_scr[g])
        for vb in range(NVB):
            for kc in range(HEAD_DIM):
                st_ref[g, vb, kc] = st_ref[g, vb, kc] * vec_scr[g, 0, kc:kc + 1, :]


def _wkv(rw, lw, lb, rk, kkc, kac, tc):
    _, S, R, _ = rw.shape
    N = HEAD_DIM
    inp = lambda n: pl.BlockSpec((None, tc, LANES, LANES), lambda g, c: (n, c, g, 0))
    const = pl.BlockSpec((2, N, LANES), lambda g, c: (0, 0, 0))
    return pl.pallas_call(
        _wkv_kernel,
        grid=(R // LANES, S // tc),
        in_specs=[inp(n) for n in range(RW_STREAMS)] + [const] * 5,
        out_specs=pl.BlockSpec((tc, LANES, LANES), lambda g, c: (c, g, 0)),
        out_shape=jax.ShapeDtypeStruct((S, R, LANES), F32),
        scratch_shapes=[pltpu.VMEM((2, NVB, N, SUBLANES, LANES), F32),
                        pltpu.VMEM((2, N, LANES), F32),
                        pltpu.VMEM((RW_STREAMS, tc, LANES, LANES), F32),
                        pltpu.VMEM((tc, LANES, LANES), F32), pltpu.VMEM((2, N, LANES), F32),
                        pltpu.VMEM((2, 4, N, LANES), F32)],
        compiler_params=pltpu.CompilerParams(
            dimension_semantics=("parallel", "arbitrary"), vmem_limit_bytes=VMEM_LIMIT),
        name="wkv",
    )(*([rw] * RW_STREAMS), lw, lb, rk, kkc, kac)


def _merge_kernel(x_ref, osb_ref, y_ref, g_ref, gpre_ref, wg_ref, bg_ref, wsb_ref, wrw_ref,
                  wo_ref, gpost_ref, x1_ref, h_scr, orw_scr, m_scr, f_scr):
    x = x_ref[...]
    h_scr[...] = _rms(x, gpre_ref[...]).astype(BF16)
    orw_scr[...] = (y_ref[...] * g_ref[...].astype(F32)).astype(BF16)
    for c0 in range(0, D_MODEL, MXU_N):
        cs = slice(c0, c0 + MXU_N)
        cr = slice(D_MODEL + c0, D_MODEL + c0 + MXU_N)
        g_sb = _sigmoid(jnp.dot(h_scr[...], wg_ref[:, cs], preferred_element_type=F32)
                        + bg_ref[:, cs])
        g_rw = _sigmoid(jnp.dot(h_scr[...], wg_ref[:, cr], preferred_element_type=F32)
                        + bg_ref[:, cr])
        a_sb = jnp.dot(osb_ref[...], wsb_ref[:, cs], preferred_element_type=F32)
        a_rw = jnp.dot(orw_scr[...], wrw_ref[:, cs], preferred_element_type=F32)
        m_scr[:, cs] = (g_sb * a_sb + g_rw * a_rw).astype(BF16)
    for c0 in range(0, D_MODEL, MXU_N):
        cs = slice(c0, c0 + MXU_N)
        f_scr[:, cs] = jnp.dot(m_scr[...], wo_ref[:, cs], preferred_element_type=F32)
    x1_ref[...] = x + _rms(f_scr[...], gpost_ref[...])


def _merge(xf, o_sb, y, g, gpre, wg, bg, wsb, wrw, wo, gpost, tm):
    T = xf.shape[0]
    spb = y.shape[0] // tm
    tile = lambda c: pl.BlockSpec((tm, c), lambda i: (i, 0))
    y_tile = pl.BlockSpec((tm, WIDTH), lambda i: (i % spb, i // spb))
    full = _resident
    return pl.pallas_call(
        _merge_kernel,
        grid=(T // tm,),
        in_specs=[tile(D_MODEL), tile(WIDTH), y_tile, tile(WIDTH), full(gpre), full(wg),
                  full(bg), full(wsb), full(wrw), full(wo), full(gpost)],
        out_specs=tile(D_MODEL),
        out_shape=jax.ShapeDtypeStruct((T, D_MODEL), F32),
        scratch_shapes=[pltpu.VMEM((tm, D_MODEL), BF16), pltpu.VMEM((tm, WIDTH), BF16),
                        pltpu.VMEM((tm, D_MODEL), BF16), pltpu.VMEM((tm, D_MODEL), F32)],
        compiler_params=pltpu.CompilerParams(
            dimension_semantics=("parallel",), vmem_limit_bytes=VMEM_LIMIT),
        name="merge",
    )(xf, o_sb, y, g, gpre, wg, bg, wsb, wrw, wo, gpost)


def _ffn_kernel(x_ref, gpre_ref, wg_ref, wu_ref, wd_ref, gpost_ref, o_ref, h_scr, acc_scr):
    x = x_ref[...]
    h_scr[...] = _rms(x, gpre_ref[...]).astype(BF16)
    for c0 in range(0, D_FF, MXU_N):
        cs = slice(c0, c0 + MXU_N)
        gt = jnp.dot(h_scr[...], wg_ref[:, cs], preferred_element_type=F32)
        up = jnp.dot(h_scr[...], wu_ref[:, cs], preferred_element_type=F32)
        act = (gt * _sigmoid(gt) * up).astype(BF16)
        part = jnp.dot(act, wd_ref[cs, :], preferred_element_type=F32)
        if c0 == 0:
            acc_scr[...] = part
        else:
            acc_scr[...] += part
    o_ref[...] = x + _rms(acc_scr[...], gpost_ref[...])


def _ffn(x1, gpre, wg, wu, wd, gpost, tm):
    T = x1.shape[0]
    tile = pl.BlockSpec((tm, D_MODEL), lambda i: (i, 0))
    full = _resident
    return pl.pallas_call(
        _ffn_kernel,
        grid=(T // tm,),
        in_specs=[tile, full(gpre), full(wg), full(wu), full(wd), full(gpost)],
        out_specs=tile,
        out_shape=jax.ShapeDtypeStruct((T, D_MODEL), F32),
        scratch_shapes=[pltpu.VMEM((tm, D_MODEL), BF16), pltpu.VMEM((tm, D_MODEL), F32)],
        compiler_params=pltpu.CompilerParams(
            dimension_semantics=("parallel",), vmem_limit_bytes=VMEM_LIMIT),
        name="ffn",
    )(x1, gpre, wg, wu, wd, gpost)


def _layer(x, norm_mix_pre, w_in, b_gate, mu_rw, w0, w_up, a0, a_up, g_up, k_k, k_a, r_k,
           lnx_w, lnx_b, w_sb_out, w_rw_out, w_o, norm_mix_post, norm_ffn_pre,
           w_ffn_gate, w_ffn_up, w_ffn_down, norm_ffn_post):
    B, S, D = x.shape
    T = B * S
    NS = B * HEADS
    assert D == D_MODEL and S % SB_TILE == 0 and (NS // 2) % LANES == 0
    tm = min(TOKEN_TILE, S)
    tp = min(POST_TILE, S)
    assert S % tm == 0 and S % tp == 0
    row = lambda a: a.reshape(1, -1)

    xf = x.reshape(T, D)
    w_in_b = w_in.astype(BF16)
    wwa = jnp.zeros((W_LORA + A_LORA, 2 * WIDTH), F32)
    wwa = wwa.at[:W_LORA, :WIDTH].set(w_up).at[W_LORA:, WIDTH:].set(a_up)
    qkv, rw, g = _in_proj(xf, row(norm_mix_pre), w_in_b[:, :MIX_COLS], row(mu_rw), row(w0),
                          row(a0), wwa, g_up, S, tm)

    o_sb = _sb_attn(qkv.reshape(B, S, SB_COLS))

    lane_head = jnp.stack([2 * (jnp.arange(LANES) % (HEADS // 2)) + p for p in range(2)])
    per_stream = lambda a: a.reshape(HEADS, HEAD_DIM)[lane_head].transpose(0, 2, 1)
    y = _wkv(rw.reshape(RW_STREAMS, S, B * WIDTH // LANES, LANES), per_stream(lnx_w),
             per_stream(lnx_b), per_stream(r_k), per_stream(k_k), per_stream(k_a), min(WKV_BLOCK, S))
    y = y.reshape(S, B * WIDTH)

    x1 = _merge(xf, o_sb.reshape(T, WIDTH), y, g, row(norm_mix_pre),
                w_in_b[:, MIX_COLS:], row(b_gate), w_sb_out.astype(BF16),
                w_rw_out.astype(BF16), w_o.astype(BF16), row(norm_mix_post), tp)
    out = _ffn(x1, row(norm_ffn_pre), w_ffn_gate.astype(BF16), w_ffn_up.astype(BF16),
               w_ffn_down.astype(BF16), row(norm_ffn_post), tp)
    return out.reshape(B, S, D)


def kernel(x, norm_mix_pre, w_in, b_gate, mu_rw, w0, w_up, a0, a_up, g_up, k_k, k_a, r_k,
           lnx_w, lnx_b, w_sb_out, w_rw_out, w_o, norm_mix_post, norm_ffn_pre,
           w_ffn_gate, w_ffn_up, w_ffn_down, norm_ffn_post):
    params = (norm_mix_pre, w_in, b_gate, mu_rw, w0, w_up, a0, a_up, g_up, k_k, k_a, r_k,
              lnx_w, lnx_b, w_sb_out, w_rw_out, w_o, norm_mix_post, norm_ffn_pre,
              w_ffn_gate, w_ffn_up, w_ffn_down, norm_ffn_post)
    for l in range(norm_mix_pre.shape[0]):
        x = _layer(x, *(p[l] for p in params))
    return x
```
